```python
import math
import jax, jax.numpy as jnp
from jax import lax
import numpy as np

D_MODEL = 2048
BATCH = 2
SEQ = 8192
DEPTH = 1

N_META = 16
D_MIX = D_MODEL
Q_BLOCK = 128
RMS_EPS = 1e-6
ROPE_THETA = 10000.0

A_HEADS = 8
A_QK_DIM = 64
A_V_DIM = 2 * A_QK_DIM
A_WIDTH = A_HEADS * A_V_DIM
B_HEADS = 8
B_Q_LORA = 512
B_KV_LORA = 256
B_NOPE = 128
B_ROPE = 64
B_V_DIM = 128
B_WIDTH = B_HEADS * B_V_DIM

A_Q_COLS = A_HEADS * 2 * A_QK_DIM
A_K_COLS = A_HEADS * 2 * A_QK_DIM
A_V_COLS = A_WIDTH
A_G_COLS = A_WIDTH
B_CQ_COLS = B_Q_LORA
B_CKV_COLS = B_KV_LORA
B_KR_COLS = B_ROPE
B_G_COLS = B_WIDTH
IN_COLS = A_Q_COLS + A_K_COLS + A_V_COLS + A_G_COLS + B_CQ_COLS + B_CKV_COLS + B_KR_COLS + B_G_COLS
UQ_COLS = B_HEADS * (B_NOPE + B_ROPE)
UKV_COLS = B_HEADS * (B_NOPE + B_V_DIM)

kernel_name = "hymba_diffattn_mla_hybrid"


def _rmsnorm(x, w):
    xf = x.astype(jnp.float32)
    xf = xf * lax.rsqrt(jnp.mean(xf * xf, axis=-1, keepdims=True) + RMS_EPS)
    return xf.astype(x.dtype) * w


def _rope(x, pos):
    half = x.shape[-1] // 2
    inv = ROPE_THETA ** (-jnp.arange(half, dtype=jnp.float32) / half)
    ang = pos.astype(jnp.float32)[:, None] * inv[None, :]
    cos, sin = jnp.cos(ang), jnp.sin(ang)
    xf = x.astype(jnp.float32)
    x1, x2 = xf[..., :half], xf[..., half:]
    return jnp.concatenate([x1 * cos - x2 * sin, x1 * sin + x2 * cos], axis=-1).astype(x.dtype)


def _sweep(block_fn, n_blocks):
    out = lax.map(block_fn, jnp.arange(n_blocks, dtype=jnp.int32) * Q_BLOCK)
    nb, b, h, qb, dv = out.shape
    return jnp.moveaxis(out, 0, 2).reshape(b, h, nb * qb, dv)


def _diff_attention(q, k, v, lam, slopes):
    L = k.shape[3]
    scale = A_QK_DIM ** -0.5
    k_pos = jnp.arange(L, dtype=jnp.int32)

    def block(start):
        qb = lax.dynamic_slice_in_dim(q, start, Q_BLOCK, axis=3)
        rel = (start + jnp.arange(Q_BLOCK, dtype=jnp.int32))[:, None] - k_pos[None, :]
        s = jnp.einsum('bhcqd,bhckd->bhcqk', qb, k).astype(jnp.float32) * scale
        s = s - slopes[:, None, None, None] * rel.astype(jnp.float32)
        s = jnp.where(rel >= 0, s, -jnp.inf)
        p = jax.nn.softmax(s, axis=-1)
        w = p[:, :, 0] - lam * p[:, :, 1]
        return jnp.einsum('bhqk,bhkd->bhqd', w.astype(v.dtype), v)

    return _sweep(block, L // Q_BLOCK)


def _mla_attention(q_nope, q_pe, k_nope, k_pe, v):
    L = k_nope.shape[2]
    scale = (B_NOPE + B_ROPE) ** -0.5
    k_pos = jnp.arange(L, dtype=jnp.int32)

    def block(start):
        qn = lax.dynamic_slice_in_dim(q_nope, start, Q_BLOCK, axis=2)
        qp = lax.dynamic_slice_in_dim(q_pe, start, Q_BLOCK, axis=2)
        rel = (start + jnp.arange(Q_BLOCK, dtype=jnp.int32))[:, None] - k_pos[None, :]
        s = (jnp.einsum('bhqd,bhkd->bhqk', qn, k_nope) + jnp.einsum('bhqd,bkd->bhqk', qp, k_pe)).astype(jnp.float32) * scale
        s = jnp.where(rel >= 0, s, -jnp.inf)
        p = jax.nn.softmax(s, axis=-1)
        return jnp.einsum('bhqk,bhkd->bhqd', p.astype(v.dtype), v)

    return _sweep(block, L // Q_BLOCK)


def setup_inputs(seed: int = 0) -> dict:
    key = jax.random.key(seed)
    ks = jax.random.split(key, 13)
    f32 = jnp.float32
    n = lambda k, shp: jax.random.normal(k, shp, dtype=f32)
    return {
        "x": n(ks[0], (BATCH, SEQ, D_MODEL)),
        "meta_tokens": n(ks[1], (N_META, D_MODEL)),
        "attn_norm_w": 1.0 + 0.02 * n(ks[2], (DEPTH, D_MODEL)),
        "w_in": n(ks[3], (DEPTH, D_MODEL, IN_COLS)) * D_MODEL ** -0.5,
        "diff_lambda": 0.1 * n(ks[4], (DEPTH, 4, A_QK_DIM)),
        "diff_subln_w": 1.0 + 0.02 * n(ks[5], (DEPTH, A_V_DIM)),
        "mla_q_norm_w": 1.0 + 0.02 * n(ks[6], (DEPTH, B_Q_LORA)),
        "w_uq": n(ks[7], (DEPTH, B_Q_LORA, UQ_COLS)) * B_Q_LORA ** -0.5,
        "mla_kv_norm_w": 1.0 + 0.02 * n(ks[8], (DEPTH, B_KV_LORA)),
        "w_ukv": n(ks[9], (DEPTH, B_KV_LORA, UKV_COLS)) * B_KV_LORA ** -0.5,
        "w_out": n(ks[10], (DEPTH, D_MIX, D_MODEL)) * D_MIX ** -0.5,
        "final_norm_w": 1.0 + 0.02 * n(ks[11], (D_MODEL,)),
    }


def reference(x, meta_tokens, attn_norm_w, w_in, diff_lambda, diff_subln_w,
              mla_q_norm_w, w_uq, mla_kv_norm_w, w_ukv, w_out, final_norm_w):
    B, S, D = x.shape
    L = S + N_META
    L_pad = -(-L // Q_BLOCK) * Q_BLOCK
    meta = jnp.broadcast_to(meta_tokens[None].astype(x.dtype), (B, N_META, D))
    h = jnp.concatenate([meta, x], axis=1)
    h = jnp.pad(h, ((0, 0), (0, L_pad - L), (0, 0)))
    pos = jnp.arange(L_pad, dtype=jnp.int32)
    slopes = jnp.exp2(-8.0 * jnp.arange(1, A_HEADS + 1, dtype=jnp.float32) / A_HEADS)
    splits = np.cumsum([A_Q_COLS, A_K_COLS, A_V_COLS, A_G_COLS, B_CQ_COLS, B_CKV_COLS, B_KR_COLS])

    for l in range(DEPTH):
        hn = _rmsnorm(h, attn_norm_w[l])
        proj = hn @ w_in[l]
        a_q, a_k, a_v, a_g, b_cq, b_ckv, b_kr, b_g = jnp.split(proj, list(splits), axis=-1)

        qA = a_q.reshape(B, L_pad, A_HEADS, 2, A_QK_DIM).transpose(0, 2, 3, 1, 4)
        kA = a_k.reshape(B, L_pad, A_HEADS, 2, A_QK_DIM).transpose(0, 2, 3, 1, 4)
        vA = a_v.reshape(B, L_pad, A_HEADS, A_V_DIM).transpose(0, 2, 1, 3)
        lam_init = 0.8 - 0.6 * math.exp(-0.3 * l)
        lp = diff_lambda[l].astype(jnp.float32)
        lam = jnp.exp(jnp.sum(lp[0] * lp[1])) - jnp.exp(jnp.sum(lp[2] * lp[3])) + lam_init
        oA = _diff_attention(qA, kA, vA, lam, slopes)
        oA = _rmsnorm(oA, diff_subln_w[l]) * (1.0 - lam_init)
        oA = oA.transpose(0, 2, 1, 3).reshape(B, L_pad, A_WIDTH) * jax.nn.silu(a_g)

        qB = (_rmsnorm(b_cq, mla_q_norm_w[l]) @ w_uq[l]).reshape(B, L_pad, B_HEADS, B_NOPE + B_ROPE).transpose(0, 2, 1, 3)
        q_nope, q_pe = qB[..., :B_NOPE], _rope(qB[..., B_NOPE:], pos)
        kvB = (_rmsnorm(b_ckv, mla_kv_norm_w[l]) @ w_ukv[l]).reshape(B, L_pad, B_HEADS, B_NOPE + B_V_DIM).transpose(0, 2, 1, 3)
        k_nope, vB = kvB[..., :B_NOPE], kvB[..., B_NOPE:]
        k_pe = _rope(b_kr, pos)
        oB = _mla_attention(q_nope, q_pe, k_nope, k_pe, vB)
        oB = oB.transpose(0, 2, 1, 3).reshape(B, L_pad, B_WIDTH) * jax.nn.silu(b_g)

        h = h + jnp.concatenate([oA, oB], axis=-1) @ w_out[l]

    y = _rmsnorm(h, final_norm_w)
    return y[:, N_META:N_META + S]
```

```python
import functools
import math

import jax
import jax.numpy as jnp
from jax import lax
from jax.experimental import pallas as pl
from jax.experimental.pallas import tpu as pltpu

N_META = 16
RMS_EPS = 1e-6
ROPE_THETA = 10000.0

A_HEADS = 8
A_QK_DIM = 64
A_V_DIM = 128
A_WIDTH = A_HEADS * A_V_DIM
B_HEADS = 8
B_Q_LORA = 512
B_KV_LORA = 256
B_NOPE = 128
B_ROPE = 64
B_V_DIM = 128
B_WIDTH = B_HEADS * B_V_DIM
B_QK_PAD = 256

LANES = 128
META_PAD = 128
VMEM_LIMIT = 48 * 1024 * 1024

_NT = (((1,), (1,)), ((), ()))


def _cparams(n_grid):
    return pltpu.CompilerParams(
        dimension_semantics=("arbitrary",) * n_grid, vmem_limit_bytes=VMEM_LIMIT)


def _norm_matmul_kernel(x_ref, nw_ref, w_ref, o_ref, xn_ref):
    @pl.when(pl.program_id(1) == 0)
    def _():
        x = x_ref[...]
        r = lax.rsqrt(jnp.mean(x * x, axis=-1, keepdims=True) + RMS_EPS)
        xn_ref[...] = ((x * r) * nw_ref[...]).astype(xn_ref.dtype)

    o_ref[...] = jnp.dot(xn_ref[...], w_ref[...],
                         preferred_element_type=jnp.float32).astype(o_ref.dtype)


def _norm_matmul(x, nw, w, out_dtype, tm, tn):
    rows, k = x.shape
    n = w.shape[1]
    return pl.pallas_call(
        _norm_matmul_kernel,
        grid=(rows // tm, n // tn),
        in_specs=[
            pl.BlockSpec((tm, k), lambda i, j: (i, 0)),
            pl.BlockSpec((1, k), lambda i, j: (0, 0)),
            pl.BlockSpec((k, tn), lambda i, j: (0, j)),
        ],
        out_specs=pl.BlockSpec((tm, tn), lambda i, j: (i, j)),
        out_shape=jax.ShapeDtypeStruct((rows, n), out_dtype),
        scratch_shapes=[pltpu.VMEM((tm, k), jnp.bfloat16)],
        compiler_params=_cparams(2),
        name="norm_matmul",
    )(x, nw, w)


def _mla_proj_kernel(small_ref, qnw_ref, kvnw_ref, inv_ref, wq_ref, wqr_ref, wk_ref, wv_ref,
                     q_ref, k_ref, v_ref, *, tm, seq, pos0, scale):
    def rms(x, w):
        r = lax.rsqrt(jnp.mean(x * x, axis=-1, keepdims=True) + RMS_EPS)
        return (x * r) * w

    cq = rms(small_ref[:, 0:B_Q_LORA], qnw_ref[...]).astype(jnp.bfloat16)
    ckv = rms(small_ref[:, B_Q_LORA:B_Q_LORA + B_KV_LORA], kvnw_ref[...]).astype(jnp.bfloat16)
    kr = small_ref[:, 768:896]
    kr_rot = small_ref[:, 896:1024]

    row = lax.broadcasted_iota(jnp.int32, (tm, LANES), 0)
    pos = (row + ((pl.program_id(0) * tm) % seq + pos0)).astype(jnp.float32)
    ang = pos * inv_ref[...]
    cos, sin = jnp.cos(ang), jnp.sin(ang)

    q1 = jnp.dot(cq, wq_ref[...], preferred_element_type=jnp.float32)
    q2 = jnp.dot(cq, wqr_ref[...], preferred_element_type=jnp.float32)
    kn = jnp.dot(ckv, wk_ref[...], preferred_element_type=jnp.float32)
    v_ref[...] = jnp.dot(ckv, wv_ref[...], preferred_element_type=jnp.float32).astype(v_ref.dtype)
    kpe = (kr * cos + kr_rot * sin).astype(k_ref.dtype)
    for h in range(B_HEADS):
        c0 = h * B_QK_PAD
        q_ref[:, c0:c0 + LANES] = (q1[:, c0:c0 + LANES] * scale).astype(q_ref.dtype)
        qpe = q1[:, c0 + LANES:c0 + 2 * LANES] * cos + q2[:, h * LANES:(h + 1) * LANES] * sin
        q_ref[:, c0 + LANES:c0 + 2 * LANES] = (qpe * scale).astype(q_ref.dtype)
        k_ref[:, c0:c0 + LANES] = kn[:, h * LANES:(h + 1) * LANES].astype(k_ref.dtype)
        k_ref[:, c0 + LANES:c0 + 2 * LANES] = kpe


def _mla_proj(small, qnw, kvnw, inv, wq, wqr, wk, wv, tm, seq, pos0):
    rows = small.shape[0]
    full = lambda a: pl.BlockSpec(a.shape, lambda i: (0, 0))
    scale = (B_NOPE + B_ROPE) ** -0.5
    return pl.pallas_call(
        functools.partial(_mla_proj_kernel, tm=tm, seq=seq, pos0=pos0, scale=scale),
        grid=(rows // tm,),
        in_specs=[pl.BlockSpec((tm, small.shape[1]), lambda i: (i, 0)),
                  full(qnw), full(kvnw), full(inv), full(wq), full(wqr), full(wk), full(wv)],
        out_specs=[pl.BlockSpec((tm, B_HEADS * B_QK_PAD), lambda i: (i, 0)),
                   pl.BlockSpec((tm, B_HEADS * B_QK_PAD), lambda i: (i, 0)),
                   pl.BlockSpec((tm, B_WIDTH), lambda i: (i, 0))],
        out_shape=[jax.ShapeDtypeStruct((rows, B_HEADS * B_QK_PAD), jnp.bfloat16),
                   jax.ShapeDtypeStruct((rows, B_HEADS * B_QK_PAD), jnp.bfloat16),
                   jax.ShapeDtypeStruct((rows, B_WIDTH), jnp.bfloat16)],
        compiler_params=_cparams(1),
        name="mla_proj",
    )(small, qnw, kvnw, inv, wq, wqr, wk, wv)


def _silu(g):
    return g * (1.0 / (1.0 + jnp.exp(-g)))


def _online_step(s, off, v, m_ref, l_ref, acc_ref):
    m_prev = m_ref[...]
    m_new = jnp.maximum(m_prev, jnp.max(s, axis=-1, keepdims=True) + off)
    alpha = jnp.exp(m_prev - m_new)
    p = jnp.exp(s + (off - m_new))
    l_ref[...] = alpha * l_ref[...] + jnp.sum(p, axis=-1, keepdims=True)
    acc_ref[...] = alpha * acc_ref[...] + jnp.dot(p.astype(v.dtype), v,
                                                  preferred_element_type=jnp.float32)
    m_ref[...] = m_new


def _first_step(s, off, v, m_ref, l_ref, acc_ref):
    m_new = jnp.max(s, axis=-1, keepdims=True) + off
    p = jnp.exp(s + (off - m_new))
    l_ref[...] = jnp.sum(p, axis=-1, keepdims=True)
    acc_ref[...] = jnp.dot(p.astype(v.dtype), v, preferred_element_type=jnp.float32)
    m_ref[...] = m_new


def _diff_attn_kernel(slopes_ref, q_ref, k_ref, v_ref, km_ref, vm_ref, g_ref, lam_ref, sw_ref,
                      o_ref, bias_ref, m_ref, l_ref, acc_ref, *, tq, tk, lam_init):
    h = pl.program_id(1)
    t = pl.program_id(2)
    rows = 2 * tq
    slope = slopes_ref[h]
    q_start = t * tq

    @pl.when(t == 0)
    def _():
        r = lax.broadcasted_iota(jnp.int32, (rows, tk), 0)
        r = jnp.where(r >= tq, r - tq, r)
        c = lax.broadcasted_iota(jnp.int32, (rows, tk), 1)
        bias_ref[...] = (r - c).astype(jnp.float32) * (-slope)

    qb = q_ref[0]
    lane = lax.broadcasted_iota(jnp.int32, (tq, LANES), 1)
    zero = jnp.zeros_like(qb)
    qs = jnp.concatenate([jnp.where(lane < A_QK_DIM, qb, zero),
                          jnp.where(lane >= A_QK_DIM, qb, zero)], axis=0)

    s = lax.dot_general(qs, km_ref[...], _NT, preferred_element_type=jnp.float32)
    s = s + bias_ref[:, 0:META_PAD]
    cm = lax.broadcasted_iota(jnp.int32, (rows, META_PAD), 1)
    s = jnp.where(cm < N_META, s, -jnp.inf)
    off = -slope * (q_start + N_META).astype(jnp.float32)
    _first_step(s, off, vm_ref[...], m_ref, l_ref, acc_ref)

    n_full = q_start // tk

    def body(kt, carry):
        k0 = pl.multiple_of(kt * tk, tk)
        s = lax.dot_general(qs, k_ref[0, pl.ds(k0, tk), :], _NT,
                            preferred_element_type=jnp.float32)
        s = s + bias_ref[...]
        off = -slope * (q_start - k0).astype(jnp.float32)
        _online_step(s, off, v_ref[0, pl.ds(k0, tk), :], m_ref, l_ref, acc_ref)
        return carry

    lax.fori_loop(0, n_full, body, 0)

    k0 = pl.multiple_of(n_full * tk, tk)
    s = lax.dot_general(qs, k_ref[0, pl.ds(k0, tk), :], _NT, preferred_element_type=jnp.float32)
    s = s + bias_ref[...]
    r = lax.broadcasted_iota(jnp.int32, (rows, tk), 0)
    r = jnp.where(r >= tq, r - tq, r)
    c = lax.broadcasted_iota(jnp.int32, (rows, tk), 1)
    s = jnp.where(c - r <= q_start - k0, s, -jnp.inf)
    off = -slope * (q_start - k0).astype(jnp.float32)
    _online_step(s, off, v_ref[0, pl.ds(k0, tk), :], m_ref, l_ref, acc_ref)

    lp = lam_ref[...]
    lam = (jnp.exp(jnp.sum(lp[0:1] * lp[1:2], axis=-1, keepdims=True))
           - jnp.exp(jnp.sum(lp[2:3] * lp[3:4], axis=-1, keepdims=True)) + lam_init)
    o = acc_ref[...] / l_ref[...]
    o = o[0:tq] - lam * o[tq:rows]
    o = (o * lax.rsqrt(jnp.mean(o * o, axis=-1, keepdims=True) + RMS_EPS)) * sw_ref[...]
    o = o * (1.0 - lam_init)
    o_ref[0] = (o * _silu(g_ref[0])).astype(o_ref.dtype)


def _diff_attn(slopes, qkv, kv_meta, gates, lam_p, subln_w, tq, tk, lam_init):
    b, s, _ = qkv.shape
    hq = A_HEADS
    return pl.pallas_call(
        functools.partial(_diff_attn_kernel, tq=tq, tk=tk, lam_init=lam_init),
        grid=(b, hq, s // tq),
        in_specs=[
            pl.BlockSpec(memory_space=pltpu.SMEM),
            pl.BlockSpec((1, tq, LANES), lambda b, h, t: (b, t, h)),
            pl.BlockSpec((1, s, LANES), lambda b, h, t: (b, 0, hq + h)),
            pl.BlockSpec((1, s, LANES), lambda b, h, t: (b, 0, 2 * hq + h)),
            pl.BlockSpec((META_PAD, LANES), lambda b, h, t: (0, hq + h)),
            pl.BlockSpec((META_PAD, LANES), lambda b, h, t: (0, 2 * hq + h)),
            pl.BlockSpec((1, tq, LANES), lambda b, h, t: (b, t, h)),
            pl.BlockSpec(lam_p.shape, lambda b, h, t: (0, 0)),
            pl.BlockSpec(subln_w.shape, lambda b, h, t: (0, 0)),
        ],
        out_specs=pl.BlockSpec((1, tq, LANES), lambda b, h, t: (b, t, h)),
        out_shape=jax.ShapeDtypeStruct((b, s, A_WIDTH), jnp.bfloat16),
        scratch_shapes=[pltpu.VMEM((2 * tq, tk), jnp.float32),
                        pltpu.VMEM((2 * tq, 1), jnp.float32),
                        pltpu.VMEM((2 * tq, 1), jnp.float32),
                        pltpu.VMEM((2 * tq, A_V_DIM), jnp.float32)],
        compiler_params=_cparams(3),
        name="diff_attn",
    )(slopes, qkv, qkv, qkv, kv_meta, kv_meta, gates, lam_p, subln_w)


def _mla_attn_kernel(q_ref, k_ref, v_ref, km_ref, vm_ref, g_ref, o_ref, m_ref, l_ref, acc_ref,
                     *, tq, tk):
    t = pl.program_id(2)
    q_start = t * tq
    q = q_ref[0]
    zero_off = jnp.float32(0.0)

    s = lax.dot_general(q, km_ref[...], _NT, preferred_element_type=jnp.float32)
    cm = lax.broadcasted_iota(jnp.int32, (tq, META_PAD), 1)
    s = jnp.where(cm < N_META, s, -jnp.inf)
    _first_step(s, zero_off, vm_ref[...], m_ref, l_ref, acc_ref)

    n_full = q_start // tk

    def body(kt, carry):
        k0 = pl.multiple_of(kt * tk, tk)
        s = lax.dot_general(q, k_ref[0, pl.ds(k0, tk), :], _NT,
                            preferred_element_type=jnp.float32)
        _online_step(s, zero_off, v_ref[0, pl.ds(k0, tk), :], m_ref, l_ref, acc_ref)
        return carry

    lax.fori_loop(0, n_full, body, 0)

    k0 = pl.multiple_of(n_full * tk, tk)
    s = lax.dot_general(q, k_ref[0, pl.ds(k0, tk), :], _NT, preferred_element_type=jnp.float32)
    r = lax.broadcasted_iota(jnp.int32, (tq, tk), 0)
    c = lax.broadcasted_iota(jnp.int32, (tq, tk), 1)
    s = jnp.where(c - r <= q_start - k0, s, -jnp.inf)
    _online_step(s, zero_off, v_ref[0, pl.ds(k0, tk), :], m_ref, l_ref, acc_ref)

    o = acc_ref[...] / l_ref[...]
    o_ref[0] = (o * _silu(g_ref[0])).astype(o_ref.dtype)


def _mla_attn(q_cat, k_cat, v, km, vm, gates, tq, tk):
    b, s, _ = q_cat.shape
    g_blk0 = A_WIDTH // LANES
    return pl.pallas_call(
        functools.partial(_mla_attn_kernel, tq=tq, tk=tk),
        grid=(b, B_HEADS, s // tq),
        in_specs=[
            pl.BlockSpec((1, tq, B_QK_PAD), lambda b, h, t: (b, t, h)),
            pl.BlockSpec((1, s, B_QK_PAD), lambda b, h, t: (b, 0, h)),
            pl.BlockSpec((1, s, B_V_DIM), lambda b, h, t: (b, 0, h)),
            pl.BlockSpec((META_PAD, B_QK_PAD), lambda b, h, t: (0, h)),
            pl.BlockSpec((META_PAD, B_V_DIM), lambda b, h, t: (0, h)),
            pl.BlockSpec((1, tq, LANES), lambda b, h, t: (b, t, g_blk0 + h)),
        ],
        out_specs=pl.BlockSpec((1, tq, B_V_DIM), lambda b, h, t: (b, t, h)),
        out_shape=jax.ShapeDtypeStruct((b, s, B_WIDTH), jnp.bfloat16),
        scratch_shapes=[pltpu.VMEM((tq, 1), jnp.float32),
                        pltpu.VMEM((tq, 1), jnp.float32),
                        pltpu.VMEM((tq, B_V_DIM), jnp.float32)],
        compiler_params=_cparams(3),
        name="mla_attn",
    )(q_cat, k_cat, v, km, vm, gates)


def _out_proj_kernel(oa_ref, ob_ref, x_ref, wa_ref, wb_ref, fw_ref, y_ref):
    d = jnp.dot(oa_ref[...], wa_ref[...], preferred_element_type=jnp.float32)
    d = d + jnp.dot(ob_ref[...], wb_ref[...], preferred_element_type=jnp.float32)
    hres = x_ref[...] + d
    r = lax.rsqrt(jnp.mean(hres * hres, axis=-1, keepdims=True) + RMS_EPS)
    y_ref[...] = (hres * r) * fw_ref[...]


def _out_proj(oa, ob, x, wa, wb, fw, tm):
    rows, d = x.shape
    full = lambda a: pl.BlockSpec(a.shape, lambda i: (0, 0))
    return pl.pallas_call(
        _out_proj_kernel,
        grid=(rows // tm,),
        in_specs=[pl.BlockSpec((tm, oa.shape[1]), lambda i: (i, 0)),
                  pl.BlockSpec((tm, ob.shape[1]), lambda i: (i, 0)),
                  pl.BlockSpec((tm, d), lambda i: (i, 0)),
                  full(wa), full(wb), full(fw)],
        out_specs=pl.BlockSpec((tm, d), lambda i: (i, 0)),
        out_shape=jax.ShapeDtypeStruct((rows, d), jnp.float32),
        compiler_params=_cparams(1),
        name="out_proj",
    )(oa, ob, x, wa, wb, fw)


def _rot_half_cols(w):
    half = w.shape[-1] // 2
    return jnp.concatenate([-w[..., half:], w[..., :half]], axis=-1)


def _pad_cols(w, width):
    return jnp.pad(w, ((0, 0), (0, width - w.shape[1])))


def kernel(x, meta_tokens, attn_norm_w, w_in, diff_lambda, diff_subln_w, mla_q_norm_w, w_uq,
           mla_kv_norm_w, w_ukv, w_out, final_norm_w):
    bsz, seq, d = x.shape
    bf16 = jnp.bfloat16
    l = 0
    lam_init = 0.8 - 0.6 * math.exp(-0.3 * l)

    wi = w_in[l]
    a_q, a_k, a_v, a_g, b_cq, b_ckv, b_kr, b_g = jnp.split(
        wi, [1024, 2048, 3072, 4096, 4608, 4864, 4928], axis=1)
    w_qkv = jnp.concatenate([a_q * (A_QK_DIM ** -0.5), a_k, a_v], axis=1).astype(bf16)
    w_gates = jnp.concatenate([a_g, b_g], axis=1).astype(bf16)
    w_small = jnp.concatenate([b_cq, b_ckv, _pad_cols(b_kr, LANES),
                               _pad_cols(_rot_half_cols(b_kr), LANES)], axis=1).astype(bf16)

    wuq = w_uq[l].reshape(B_Q_LORA, B_HEADS, B_NOPE + B_ROPE)
    wq_main = jnp.pad(wuq, ((0, 0), (0, 0), (0, B_QK_PAD - B_NOPE - B_ROPE)))
    wq_main = wq_main.reshape(B_Q_LORA, B_HEADS * B_QK_PAD).astype(bf16)
    wq_rot = jnp.pad(_rot_half_cols(wuq[..., B_NOPE:]), ((0, 0), (0, 0), (0, LANES - B_ROPE)))
    wq_rot = wq_rot.reshape(B_Q_LORA, B_HEADS * LANES).astype(bf16)
    wukv = w_ukv[l].reshape(B_KV_LORA, B_HEADS, B_NOPE + B_V_DIM)
    wk = wukv[..., :B_NOPE].reshape(B_KV_LORA, B_HEADS * B_NOPE).astype(bf16)
    wv = wukv[..., B_NOPE:].reshape(B_KV_LORA, B_WIDTH).astype(bf16)
    wo = w_out[l].astype(bf16)

    half = B_ROPE // 2
    inv = ROPE_THETA ** (-jnp.arange(half, dtype=jnp.float32) / half)
    inv = jnp.tile(inv, LANES // half)[None, :]
    slopes = jnp.exp2(-8.0 * jnp.arange(1, A_HEADS + 1, dtype=jnp.float32) / A_HEADS)

    nw = attn_norm_w[l][None, :]
    qnw = mla_q_norm_w[l][None, :]
    kvnw = mla_kv_norm_w[l][None, :]

    x2 = x.reshape(bsz * seq, d)
    qkv = _norm_matmul(x2, nw, w_qkv, bf16, 512, 1024)
    gates = _norm_matmul(x2, nw, w_gates, jnp.float32, 512, 1024)
    small = _norm_matmul(x2, nw, w_small, jnp.float32, 512, 1024)
    meta = meta_tokens.astype(x.dtype)
    qkv_m = _norm_matmul(meta, nw, w_qkv, bf16, N_META, 1024)
    small_m = _norm_matmul(meta, nw, w_small, jnp.float32, N_META, 1024)

    q_cat, k_cat, v_b = _mla_proj(small, qnw, kvnw, inv, wq_main, wq_rot, wk, wv, 512, seq, N_META)
    _, k_cat_m, v_b_m = _mla_proj(small_m, qnw, kvnw, inv, wq_main, wq_rot, wk, wv, N_META,
                                  N_META, 0)

    pad_meta = lambda a: jnp.pad(a, ((0, META_PAD - N_META), (0, 0)))

    o_a = _diff_attn(slopes, qkv.reshape(bsz, seq, -1), pad_meta(qkv_m),
                     gates.reshape(bsz, seq, -1), diff_lambda[l].astype(jnp.float32),
                     diff_subln_w[l][None, :], 256, 512, lam_init)
    o_b = _mla_attn(q_cat.reshape(bsz, seq, -1), k_cat.reshape(bsz, seq, -1),
                    v_b.reshape(bsz, seq, -1), pad_meta(k_cat_m), pad_meta(v_b_m),
                    gates.reshape(bsz, seq, -1), 512, 512)

    y = _out_proj(o_a.reshape(bsz * seq, -1), o_b.reshape(bsz * seq, -1), x2,
                  wo[:A_WIDTH], wo[A_WIDTH:], final_norm_w[None, :], 256)
    return y.reshape(bsz, seq, d)
```

```python
import functools
import math

import jax
import jax.numpy as jnp
from jax import lax
from jax.experimental import pallas as pl
from jax.experimental.pallas import tpu as pltpu

N_META = 16
RMS_EPS = 1e-6
ROPE_THETA = 10000.0
LOG2E = math.log2(math.e)

A_HEADS = 8
A_QK_DIM = 64
A_V_DIM = 128
A_WIDTH = A_HEADS * A_V_DIM
B_HEADS = 8
B_Q_LORA = 512
B_KV_LORA = 256
B_NOPE = 128
B_ROPE = 64
B_V_DIM = 128
B_WIDTH = B_HEADS * B_V_DIM
B_QK_PAD = 256

LANES = 128
META_PAD = 128
VMEM_LIMIT = 48 * 1024 * 1024

_NT = (((1,), (1,)), ((), ()))


def _cparams(n_grid):
    return pltpu.CompilerParams(
        dimension_semantics=("arbitrary",) * n_grid, vmem_limit_bytes=VMEM_LIMIT)


def _norm_matmul_kernel(x_ref, nw_ref, w_ref, o_ref, xn_ref):
    @pl.when(pl.program_id(1) == 0)
    def _():
        x = x_ref[...]
        r = lax.rsqrt(jnp.mean(x * x, axis=-1, keepdims=True) + RMS_EPS)
        xn_ref[...] = ((x * r) * nw_ref[...]).astype(xn_ref.dtype)

    o_ref[...] = jnp.dot(xn_ref[...], w_ref[...],
                         preferred_element_type=jnp.float32).astype(o_ref.dtype)


def _norm_matmul(x, nw, w, out_dtype, tm, tn):
    rows, k = x.shape
    n = w.shape[1]
    return pl.pallas_call(
        _norm_matmul_kernel,
        grid=(rows // tm, n // tn),
        in_specs=[
            pl.BlockSpec((tm, k), lambda i, j: (i, 0)),
            pl.BlockSpec((1, k), lambda i, j: (0, 0)),
            pl.BlockSpec((k, tn), lambda i, j: (0, j)),
        ],
        out_specs=pl.BlockSpec((tm, tn), lambda i, j: (i, j)),
        out_shape=jax.ShapeDtypeStruct((rows, n), out_dtype),
        scratch_shapes=[pltpu.VMEM((tm, k), jnp.bfloat16)],
        compiler_params=_cparams(2),
        name="norm_matmul",
    )(x, nw, w)


def _mla_proj_kernel(small_ref, qnw_ref, kvnw_ref, inv_ref, wq_ref, wqr_ref, wk_ref, wv_ref,
                     q_ref, k_ref, v_ref, *, tm, seq, pos0, scale):
    def rms(x, w):
        r = lax.rsqrt(jnp.mean(x * x, axis=-1, keepdims=True) + RMS_EPS)
        return (x * r) * w

    cq = rms(small_ref[:, 0:B_Q_LORA], qnw_ref[...]).astype(jnp.bfloat16)
    ckv = rms(small_ref[:, B_Q_LORA:B_Q_LORA + B_KV_LORA], kvnw_ref[...]).astype(jnp.bfloat16)
    kr = small_ref[:, 768:896]
    kr_rot = small_ref[:, 896:1024]

    row = lax.broadcasted_iota(jnp.int32, (tm, LANES), 0)
    pos = (row + ((pl.program_id(0) * tm) % seq + pos0)).astype(jnp.float32)
    ang = pos * inv_ref[...]
    cos, sin = jnp.cos(ang), jnp.sin(ang)

    q1 = jnp.dot(cq, wq_ref[...], preferred_element_type=jnp.float32)
    q2 = jnp.dot(cq, wqr_ref[...], preferred_element_type=jnp.float32)
    kn = jnp.dot(ckv, wk_ref[...], preferred_element_type=jnp.float32)
    v_ref[...] = jnp.dot(ckv, wv_ref[...], preferred_element_type=jnp.float32).astype(v_ref.dtype)
    kpe = (kr * cos + kr_rot * sin).astype(k_ref.dtype)
    for h in range(B_HEADS):
        c0 = h * B_QK_PAD
        q_ref[:, c0:c0 + LANES] = (q1[:, c0:c0 + LANES] * scale).astype(q_ref.dtype)
        qpe = q1[:, c0 + LANES:c0 + 2 * LANES] * cos + q2[:, h * LANES:(h + 1) * LANES] * sin
        q_ref[:, c0 + LANES:c0 + 2 * LANES] = (qpe * scale).astype(q_ref.dtype)
        k_ref[:, c0:c0 + LANES] = kn[:, h * LANES:(h + 1) * LANES].astype(k_ref.dtype)
        k_ref[:, c0 + LANES:c0 + 2 * LANES] = kpe


def _mla_proj(small, qnw, kvnw, inv, wq, wqr, wk, wv, tm, seq, pos0):
    rows = small.shape[0]
    full = lambda a: pl.BlockSpec(a.shape, lambda i: (0, 0))
    scale = (B_NOPE + B_ROPE) ** -0.5 * LOG2E
    return pl.pallas_call(
        functools.partial(_mla_proj_kernel, tm=tm, seq=seq, pos0=pos0, scale=scale),
        grid=(rows // tm,),
        in_specs=[pl.BlockSpec((tm, small.shape[1]), lambda i: (i, 0)),
                  full(qnw), full(kvnw), full(inv), full(wq), full(wqr), full(wk), full(wv)],
        out_specs=[pl.BlockSpec((tm, B_HEADS * B_QK_PAD), lambda i: (i, 0)),
                   pl.BlockSpec((tm, B_HEADS * B_QK_PAD), lambda i: (i, 0)),
                   pl.BlockSpec((tm, B_WIDTH), lambda i: (i, 0))],
        out_shape=[jax.ShapeDtypeStruct((rows, B_HEADS * B_QK_PAD), jnp.bfloat16),
                   jax.ShapeDtypeStruct((rows, B_HEADS * B_QK_PAD), jnp.bfloat16),
                   jax.ShapeDtypeStruct((rows, B_WIDTH), jnp.bfloat16)],
        compiler_params=_cparams(1),
        name="mla_proj",
    )(small, qnw, kvnw, inv, wq, wqr, wk, wv)


def _silu(g):
    return g * (1.0 / (1.0 + jnp.exp(-g)))


def _lanes(x, width):
    return jnp.tile(x, (1, width // LANES))


def _online_step(s, off, v, m_ref, l_ref, acc_ref):
    tk = s.shape[-1]
    m_prev = m_ref[...]
    m_new = jnp.maximum(m_prev, jnp.max(s, axis=-1, keepdims=True) + off)
    alpha = jnp.exp2(m_prev - m_new)
    p = jnp.exp2(s + _lanes(off - m_new, tk))
    l_ref[...] = alpha * l_ref[...] + jnp.sum(p, axis=-1, keepdims=True)
    acc_ref[...] = alpha * acc_ref[...] + jnp.dot(p.astype(v.dtype), v,
                                                  preferred_element_type=jnp.float32)
    m_ref[...] = m_new


def _first_step(s, off, v, m_ref, l_ref, acc_ref):
    tk = s.shape[-1]
    m_new = jnp.broadcast_to(jnp.max(s, axis=-1, keepdims=True) + off, m_ref.shape)
    p = jnp.exp2(s + _lanes(off - m_new, tk))
    l_ref[...] = jnp.broadcast_to(jnp.sum(p, axis=-1, keepdims=True), l_ref.shape)
    acc_ref[...] = jnp.dot(p.astype(v.dtype), v, preferred_element_type=jnp.float32)
    m_ref[...] = m_new


def _diff_attn_kernel(slopes_ref, q_ref, k_ref, v_ref, km_ref, vm_ref, g_ref, lam_ref, sw_ref,
                      o_ref, bias_ref, m_ref, l_ref, acc_ref, *, tq, tk, lam_init):
    h = pl.program_id(1)
    t = pl.program_id(2)
    rows = 2 * tq
    slope = slopes_ref[h] * LOG2E
    q_start = t * tq

    @pl.when(t == 0)
    def _():
        r = lax.broadcasted_iota(jnp.int32, (rows, tk), 0)
        r = jnp.where(r >= tq, r - tq, r)
        c = lax.broadcasted_iota(jnp.int32, (rows, tk), 1)
        bias_ref[...] = (r - c).astype(jnp.float32) * (-slope)

    qb = q_ref[0]
    lane = lax.broadcasted_iota(jnp.int32, (tq, LANES), 1)
    zero = jnp.zeros_like(qb)
    qs = jnp.concatenate([jnp.where(lane < A_QK_DIM, qb, zero),
                          jnp.where(lane >= A_QK_DIM, qb, zero)], axis=0)

    s = lax.dot_general(qs, km_ref[...], _NT, preferred_element_type=jnp.float32)
    s = s + bias_ref[:, 0:META_PAD]
    cm = lax.broadcasted_iota(jnp.int32, (rows, META_PAD), 1)
    s = jnp.where(cm < N_META, s, -jnp.inf)
    off = -slope * (q_start + N_META).astype(jnp.float32)
    _first_step(s, off, vm_ref[...], m_ref, l_ref, acc_ref)

    n_full = q_start // tk

    def body(kt, carry):
        k0 = pl.multiple_of(kt * tk, tk)
        s = lax.dot_general(qs, k_ref[0, pl.ds(k0, tk), :], _NT,
                            preferred_element_type=jnp.float32)
        s = s + bias_ref[...]
        off = -slope * (q_start - k0).astype(jnp.float32)
        _online_step(s, off, v_ref[0, pl.ds(k0, tk), :], m_ref, l_ref, acc_ref)
        return carry

    lax.fori_loop(0, n_full, body, 0)

    k0 = pl.multiple_of(n_full * tk, tk)
    s = lax.dot_general(qs, k_ref[0, pl.ds(k0, tk), :], _NT, preferred_element_type=jnp.float32)
    s = s + bias_ref[...]
    r = lax.broadcasted_iota(jnp.int32, (rows, tk), 0)
    r = jnp.where(r >= tq, r - tq, r)
    c = lax.broadcasted_iota(jnp.int32, (rows, tk), 1)
    s = jnp.where(c - r <= q_start - k0, s, -jnp.inf)
    off = -slope * (q_start - k0).astype(jnp.float32)
    _online_step(s, off, v_ref[0, pl.ds(k0, tk), :], m_ref, l_ref, acc_ref)

    lp = lam_ref[...]
    lam = (jnp.exp(jnp.sum(lp[0:1] * lp[1:2], axis=-1, keepdims=True))
           - jnp.exp(jnp.sum(lp[2:3] * lp[3:4], axis=-1, keepdims=True)) + lam_init)
    o = acc_ref[...] / l_ref[...]
    o = o[0:tq] - lam * o[tq:rows]
    o = (o * lax.rsqrt(jnp.mean(o * o, axis=-1, keepdims=True) + RMS_EPS)) * sw_ref[...]
    o = o * (1.0 - lam_init)
    o_ref[0] = (o * _silu(g_ref[0])).astype(o_ref.dtype)


def _diff_attn(slopes, qkv, kv_meta, gates, lam_p, subln_w, tq, tk, lam_init):
    b, s, _ = qkv.shape
    hq = A_HEADS
    return pl.pallas_call(
        functools.partial(_diff_attn_kernel, tq=tq, tk=tk, lam_init=lam_init),
        grid=(b, hq, s // tq),
        in_specs=[
            pl.BlockSpec(memory_space=pltpu.SMEM),
            pl.BlockSpec((1, tq, LANES), lambda b, h, t: (b, t, h)),
            pl.BlockSpec((1, s, LANES), lambda b, h, t: (b, 0, hq + h)),
            pl.BlockSpec((1, s, LANES), lambda b, h, t: (b, 0, 2 * hq + h)),
            pl.BlockSpec((META_PAD, LANES), lambda b, h, t: (0, hq + h)),
            pl.BlockSpec((META_PAD, LANES), lambda b, h, t: (0, 2 * hq + h)),
            pl.BlockSpec((1, tq, LANES), lambda b, h, t: (b, t, h)),
            pl.BlockSpec(lam_p.shape, lambda b, h, t: (0, 0)),
            pl.BlockSpec(subln_w.shape, lambda b, h, t: (0, 0)),
        ],
        out_specs=pl.BlockSpec((1, tq, LANES), lambda b, h, t: (b, t, h)),
        out_shape=jax.ShapeDtypeStruct((b, s, A_WIDTH), jnp.bfloat16),
        scratch_shapes=[pltpu.VMEM((2 * tq, tk), jnp.float32),
                        pltpu.VMEM((2 * tq, LANES), jnp.float32),
                        pltpu.VMEM((2 * tq, LANES), jnp.float32),
                        pltpu.VMEM((2 * tq, A_V_DIM), jnp.float32)],
        compiler_params=_cparams(3),
        name="diff_attn",
    )(slopes, qkv, qkv, qkv, kv_meta, kv_meta, gates, lam_p, subln_w)


def _mla_attn_kernel(q_ref, k_ref, v_ref, km_ref, vm_ref, g_ref, o_ref, m_ref, l_ref, acc_ref,
                     *, tq, tk):
    t = pl.program_id(2)
    q_start = t * tq
    q = q_ref[0]
    zero_off = jnp.float32(0.0)

    s = lax.dot_general(q, km_ref[...], _NT, preferred_element_type=jnp.float32)
    cm = lax.broadcasted_iota(jnp.int32, (tq, META_PAD), 1)
    s = jnp.where(cm < N_META, s, -jnp.inf)
    _first_step(s, zero_off, vm_ref[...], m_ref, l_ref, acc_ref)

    n_full = q_start // tk

    def body(kt, carry):
        k0 = pl.multiple_of(kt * tk, tk)
        s = lax.dot_general(q, k_ref[0, pl.ds(k0, tk), :], _NT,
                            preferred_element_type=jnp.float32)
        _online_step(s, zero_off, v_ref[0, pl.ds(k0, tk), :], m_ref, l_ref, acc_ref)
        return carry

    lax.fori_loop(0, n_full, body, 0)

    k0 = pl.multiple_of(n_full * tk, tk)
    s = lax.dot_general(q, k_ref[0, pl.ds(k0, tk), :], _NT, preferred_element_type=jnp.float32)
    r = lax.broadcasted_iota(jnp.int32, (tq, tk), 0)
    c = lax.broadcasted_iota(jnp.int32, (tq, tk), 1)
    s = jnp.where(c - r <= q_start - k0, s, -jnp.inf)
    _online_step(s, zero_off, v_ref[0, pl.ds(k0, tk), :], m_ref, l_ref, acc_ref)

    o = acc_ref[...] / l_ref[...]
    o_ref[0] = (o * _silu(g_ref[0])).astype(o_ref.dtype)


def _mla_attn(q_cat, k_cat, v, km, vm, gates, tq, tk):
    b, s, _ = q_cat.shape
    g_blk0 = A_WIDTH // LANES
    return pl.pallas_call(
        functools.partial(_mla_attn_kernel, tq=tq, tk=tk),
        grid=(b, B_HEADS, s // tq),
        in_specs=[
            pl.BlockSpec((1, tq, B_QK_PAD), lambda b, h, t: (b, t, h)),
            pl.BlockSpec((1, s, B_QK_PAD), lambda b, h, t: (b, 0, h)),
            pl.BlockSpec((1, s, B_V_DIM), lambda b, h, t: (b, 0, h)),
            pl.BlockSpec((META_PAD, B_QK_PAD), lambda b, h, t: (0, h)),
            pl.BlockSpec((META_PAD, B_V_DIM), lambda b, h, t: (0, h)),
            pl.BlockSpec((1, tq, LANES), lambda b, h, t: (b, t, g_blk0 + h)),
        ],
        out_specs=pl.BlockSpec((1, tq, B_V_DIM), lambda b, h, t: (b, t, h)),
        out_shape=jax.ShapeDtypeStruct((b, s, B_WIDTH), jnp.bfloat16),
        scratch_shapes=[pltpu.VMEM((tq, LANES), jnp.float32),
                        pltpu.VMEM((tq, LANES), jnp.float32),
                        pltpu.VMEM((tq, B_V_DIM), jnp.float32)],
        compiler_params=_cparams(3),
        name="mla_attn",
    )(q_cat, k_cat, v, km, vm, gates)


def _out_proj_kernel(oa_ref, ob_ref, x_ref, wa_ref, wb_ref, fw_ref, y_ref):
    d = jnp.dot(oa_ref[...], wa_ref[...], preferred_element_type=jnp.float32)
    d = d + jnp.dot(ob_ref[...], wb_ref[...], preferred_element_type=jnp.float32)
    hres = x_ref[...] + d
    r = lax.rsqrt(jnp.mean(hres * hres, axis=-1, keepdims=True) + RMS_EPS)
    y_ref[...] = (hres * r) * fw_ref[...]


def _out_proj(oa, ob, x, wa, wb, fw, tm):
    rows, d = x.shape
    full = lambda a: pl.BlockSpec(a.shape, lambda i: (0, 0))
    return pl.pallas_call(
        _out_proj_kernel,
        grid=(rows // tm,),
        in_specs=[pl.BlockSpec((tm, oa.shape[1]), lambda i: (i, 0)),
                  pl.BlockSpec((tm, ob.shape[1]), lambda i: (i, 0)),
                  pl.BlockSpec((tm, d), lambda i: (i, 0)),
                  full(wa), full(wb), full(fw)],
        out_specs=pl.BlockSpec((tm, d), lambda i: (i, 0)),
        out_shape=jax.ShapeDtypeStruct((rows, d), jnp.float32),
        compiler_params=_cparams(1),
        name="out_proj",
    )(oa, ob, x, wa, wb, fw)


def _rot_half_cols(w):
    half = w.shape[-1] // 2
    return jnp.concatenate([-w[..., half:], w[..., :half]], axis=-1)


def _pad_cols(w, width):
    return jnp.pad(w, ((0, 0), (0, width - w.shape[1])))


def kernel(x, meta_tokens, attn_norm_w, w_in, diff_lambda, diff_subln_w, mla_q_norm_w, w_uq,
           mla_kv_norm_w, w_ukv, w_out, final_norm_w):
    bsz, seq, d = x.shape
    bf16 = jnp.bfloat16
    l = 0
    lam_init = 0.8 - 0.6 * math.exp(-0.3 * l)

    wi = w_in[l]
    a_q, a_k, a_v, a_g, b_cq, b_ckv, b_kr, b_g = jnp.split(
        wi, [1024, 2048, 3072, 4096, 4608, 4864, 4928], axis=1)
    w_qkv = jnp.concatenate([a_q * (A_QK_DIM ** -0.5 * LOG2E), a_k, a_v], axis=1).astype(bf16)
    w_gates = jnp.concatenate([a_g, b_g], axis=1).astype(bf16)
    w_small = jnp.concatenate([b_cq, b_ckv, _pad_cols(b_kr, LANES),
                               _pad_cols(_rot_half_cols(b_kr), LANES)], axis=1).astype(bf16)

    wuq = w_uq[l].reshape(B_Q_LORA, B_HEADS, B_NOPE + B_ROPE)
    wq_main = jnp.pad(wuq, ((0, 0), (0, 0), (0, B_QK_PAD - B_NOPE - B_ROPE)))
    wq_main = wq_main.reshape(B_Q_LORA, B_HEADS * B_QK_PAD).astype(bf16)
    wq_rot = jnp.pad(_rot_half_cols(wuq[..., B_NOPE:]), ((0, 0), (0, 0), (0, LANES - B_ROPE)))
    wq_rot = wq_rot.reshape(B_Q_LORA, B_HEADS * LANES).astype(bf16)
    wukv = w_ukv[l].reshape(B_KV_LORA, B_HEADS, B_NOPE + B_V_DIM)
    wk = wukv[..., :B_NOPE].reshape(B_KV_LORA, B_HEADS * B_NOPE).astype(bf16)
    wv = wukv[..., B_NOPE:].reshape(B_KV_LORA, B_WIDTH).astype(bf16)
    wo = w_out[l].astype(bf16)

    half = B_ROPE // 2
    inv = ROPE_THETA ** (-jnp.arange(half, dtype=jnp.float32) / half)
    inv = jnp.tile(inv, LANES // half)[None, :]
    slopes = jnp.exp2(-8.0 * jnp.arange(1, A_HEADS + 1, dtype=jnp.float32) / A_HEADS)

    nw = attn_norm_w[l][None, :]
    qnw = mla_q_norm_w[l][None, :]
    kvnw = mla_kv_norm_w[l][None, :]

    x2 = x.reshape(bsz * seq, d)
    qkv = _norm_matmul(x2, nw, w_qkv, bf16, 512, 1024)
    gates = _norm_matmul(x2, nw, w_gates, jnp.float32, 512, 1024)
    small = _norm_matmul(x2, nw, w_small, jnp.float32, 512, 1024)
    meta = meta_tokens.astype(x.dtype)
    qkv_m = _norm_matmul(meta, nw, w_qkv, bf16, N_META, 1024)
    small_m = _norm_matmul(meta, nw, w_small, jnp.float32, N_META, 1024)

    q_cat, k_cat, v_b = _mla_proj(small, qnw, kvnw, inv, wq_main, wq_rot, wk, wv, 512, seq, N_META)
    _, k_cat_m, v_b_m = _mla_proj(small_m, qnw, kvnw, inv, wq_main, wq_rot, wk, wv, N_META,
                                  N_META, 0)

    pad_meta = lambda a: jnp.pad(a, ((0, META_PAD - N_META), (0, 0)))

    o_a = _diff_attn(slopes, qkv.reshape(bsz, seq, -1), pad_meta(qkv_m),
                     gates.reshape(bsz, seq, -1), diff_lambda[l].astype(jnp.float32),
                     diff_subln_w[l][None, :], 256, 512, lam_init)
    o_b = _mla_attn(q_cat.reshape(bsz, seq, -1), k_cat.reshape(bsz, seq, -1),
                    v_b.reshape(bsz, seq, -1), pad_meta(k_cat_m), pad_meta(v_b_m),
                    gates.reshape(bsz, seq, -1), 512, 512)

    y = _out_proj(o_a.reshape(bsz * seq, -1), o_b.reshape(bsz * seq, -1), x2,
                  wo[:A_WIDTH], wo[A_WIDTH:], final_norm_w[None, :], 256)
    return y.reshape(bsz, seq, d)
```

```python
import functools
import math

import jax
import jax.numpy as jnp
from jax import lax
from jax.experimental import pallas as pl
from jax.experimental.pallas import tpu as pltpu

N_META = 16
RMS_EPS = 1e-6
ROPE_THETA = 10000.0
LOG2E = math.log2(math.e)

A_HEADS = 8
A_QK_DIM = 64
A_V_DIM = 128
A_WIDTH = A_HEADS * A_V_DIM
B_HEADS = 8
B_Q_LORA = 512
B_KV_LORA = 256
B_NOPE = 128
B_ROPE = 64
B_V_DIM = 128
B_WIDTH = B_HEADS * B_V_DIM
B_QK_PAD = 256

LANES = 128
META_PAD = 128
POS_RADIX = 256
VMEM_LIMIT = 48 * 1024 * 1024

_NT = (((1,), (1,)), ((), ()))


def _cparams(n_grid):
    return pltpu.CompilerParams(
        dimension_semantics=("arbitrary",) * n_grid, vmem_limit_bytes=VMEM_LIMIT)


def _norm_matmul_kernel(x_ref, nw_ref, w_ref, o_ref, xn_ref):
    @pl.when(pl.program_id(1) == 0)
    def _():
        x = x_ref[...]
        r = lax.rsqrt(jnp.mean(x * x, axis=-1, keepdims=True) + RMS_EPS)
        xn_ref[...] = ((x * r) * nw_ref[...]).astype(xn_ref.dtype)

    o_ref[...] = jnp.dot(xn_ref[...], w_ref[...],
                         preferred_element_type=jnp.float32).astype(o_ref.dtype)


def _norm_matmul(x, nw, w, out_dtype, tm, tn):
    rows, k = x.shape
    n = w.shape[1]
    return pl.pallas_call(
        _norm_matmul_kernel,
        grid=(rows // tm, n // tn),
        in_specs=[
            pl.BlockSpec((tm, k), lambda i, j: (i, 0)),
            pl.BlockSpec((1, k), lambda i, j: (0, 0)),
            pl.BlockSpec((k, tn), lambda i, j: (0, j)),
        ],
        out_specs=pl.BlockSpec((tm, tn), lambda i, j: (i, j)),
        out_shape=jax.ShapeDtypeStruct((rows, n), out_dtype),
        scratch_shapes=[pltpu.VMEM((tm, k), jnp.bfloat16)],
        compiler_params=_cparams(2),
        name="norm_matmul",
    )(x, nw, w)


def _mla_proj_kernel(small_ref, qnw_ref, kvnw_ref, inv_ref, wq_ref, wqr_ref, wk_ref, wv_ref,
                     q_ref, k_ref, v_ref, *, tm, seq, pos0, scale):
    def rms(x, w):
        r = lax.rsqrt(jnp.mean(x * x, axis=-1, keepdims=True) + RMS_EPS)
        return (x * r) * w

    cq = rms(small_ref[:, 0:B_Q_LORA], qnw_ref[...]).astype(jnp.bfloat16)
    ckv = rms(small_ref[:, B_Q_LORA:B_Q_LORA + B_KV_LORA], kvnw_ref[...]).astype(jnp.bfloat16)
    kr = small_ref[:, 768:896]
    kr_rot = small_ref[:, 896:1024]

    row = lax.broadcasted_iota(jnp.int32, (tm, LANES), 0)
    pos = (row + ((pl.program_id(0) * tm) % seq + pos0)).astype(jnp.float32)
    ang = pos * inv_ref[...]
    cos, sin = jnp.cos(ang), jnp.sin(ang)

    q1 = jnp.dot(cq, wq_ref[...], preferred_element_type=jnp.float32)
    q2 = jnp.dot(cq, wqr_ref[...], preferred_element_type=jnp.float32)
    kn = jnp.dot(ckv, wk_ref[...], preferred_element_type=jnp.float32)
    v_ref[...] = jnp.dot(ckv, wv_ref[...], preferred_element_type=jnp.float32).astype(v_ref.dtype)
    kpe = (kr * cos + kr_rot * sin).astype(k_ref.dtype)
    for h in range(B_HEADS):
        c0 = h * B_QK_PAD
        q_ref[:, c0:c0 + LANES] = (q1[:, c0:c0 + LANES] * scale).astype(q_ref.dtype)
        qpe = q1[:, c0 + LANES:c0 + 2 * LANES] * cos + q2[:, h * LANES:(h + 1) * LANES] * sin
        q_ref[:, c0 + LANES:c0 + 2 * LANES] = (qpe * scale).astype(q_ref.dtype)
        k_ref[:, c0:c0 + LANES] = kn[:, h * LANES:(h + 1) * LANES].astype(k_ref.dtype)
        k_ref[:, c0 + LANES:c0 + 2 * LANES] = kpe


def _mla_proj(small, qnw, kvnw, inv, wq, wqr, wk, wv, tm, seq, pos0):
    rows = small.shape[0]
    full = lambda a: pl.BlockSpec(a.shape, lambda i: (0, 0))
    scale = (B_NOPE + B_ROPE) ** -0.5 * LOG2E
    return pl.pallas_call(
        functools.partial(_mla_proj_kernel, tm=tm, seq=seq, pos0=pos0, scale=scale),
        grid=(rows // tm,),
        in_specs=[pl.BlockSpec((tm, small.shape[1]), lambda i: (i, 0)),
                  full(qnw), full(kvnw), full(inv), full(wq), full(wqr), full(wk), full(wv)],
        out_specs=[pl.BlockSpec((tm, B_HEADS * B_QK_PAD), lambda i: (i, 0)),
                   pl.BlockSpec((tm, B_HEADS * B_QK_PAD), lambda i: (i, 0)),
                   pl.BlockSpec((tm, B_WIDTH), lambda i: (i, 0))],
        out_shape=[jax.ShapeDtypeStruct((rows, B_HEADS * B_QK_PAD), jnp.bfloat16),
                   jax.ShapeDtypeStruct((rows, B_HEADS * B_QK_PAD), jnp.bfloat16),
                   jax.ShapeDtypeStruct((rows, B_WIDTH), jnp.bfloat16)],
        compiler_params=_cparams(1),
        name="mla_proj",
    )(small, qnw, kvnw, inv, wq, wqr, wk, wv)


def _silu(g):
    return g * (1.0 / (1.0 + jnp.exp(-g)))


def _lanes(x, width):
    return jnp.tile(x, (1, width // LANES))


def _with_ones(v):
    return jnp.concatenate([v, jnp.ones((v.shape[0], LANES), v.dtype)], axis=1)


def _online_step(s, v, m_ref, acc_ref):
    tk = s.shape[-1]
    m_prev = m_ref[...]
    m_new = jnp.maximum(m_prev, jnp.max(s, axis=-1, keepdims=True))
    alpha = jnp.exp2(m_prev - m_new)
    p = jnp.exp2(s - _lanes(m_new, tk))
    pv = jnp.dot(p.astype(v.dtype), _with_ones(v), preferred_element_type=jnp.float32)
    acc_ref[...] = _lanes(alpha, acc_ref.shape[-1]) * acc_ref[...] + pv
    m_ref[...] = m_new


def _first_step(s, v, m_ref, acc_ref):
    tk = s.shape[-1]
    m_new = jnp.broadcast_to(jnp.max(s, axis=-1, keepdims=True), m_ref.shape)
    p = jnp.exp2(s - _lanes(m_new, tk))
    acc_ref[...] = jnp.dot(p.astype(v.dtype), _with_ones(v), preferred_element_type=jnp.float32)
    m_ref[...] = m_new


def _causal_flash(q_start, tk, scores, meta_scores, diag_mask,
                  v_ref, vm_ref, sa_ref, sb_ref, m_ref, acc_ref):
    n_full = q_start // tk

    def softmax_pv(s_ref, kt, masked):
        k0 = pl.multiple_of(kt * tk, tk)
        s = s_ref[...]
        if masked:
            s = jnp.where(diag_mask(k0), s, -jnp.inf)
        _online_step(s, v_ref[0, pl.ds(k0, tk), :], m_ref, acc_ref)

    sa_ref[...] = scores(0)
    _first_step(meta_scores(), vm_ref[...], m_ref, acc_ref)

    def pair(j, carry):
        sb_ref[...] = scores(2 * j + 1)
        softmax_pv(sa_ref, 2 * j, False)
        sa_ref[...] = scores(2 * j + 2)
        softmax_pv(sb_ref, 2 * j + 1, False)
        return carry

    n_pairs = n_full // 2
    lax.fori_loop(0, n_pairs, pair, 0)
    kt = 2 * n_pairs

    @pl.when(n_full > kt)
    def _():
        sb_ref[...] = scores(kt + 1)
        softmax_pv(sa_ref, kt, False)
        softmax_pv(sb_ref, kt + 1, True)

    @pl.when(n_full == kt)
    def _():
        softmax_pv(sa_ref, kt, True)


def _diff_attn_kernel(slopes_ref, q_ref, k_ref, v_ref, pk_ref, km_ref, vm_ref, pkm_ref, g_ref,
                      lam_ref, sw_ref, o_ref, sa_ref, sb_ref, m_ref, acc_ref,
                      *, tq, tk, lam_init):
    h = pl.program_id(1)
    t = pl.program_id(2)
    rows = 2 * tq
    q_start = t * tq

    slope = jnp.full((1, LANES), slopes_ref[h] * LOG2E, jnp.float32)
    s0 = slope.astype(jnp.bfloat16).astype(jnp.float32)
    s1 = (slope - s0).astype(jnp.bfloat16).astype(jnp.float32)
    s2 = (slope - s0 - s1).astype(jnp.bfloat16).astype(jnp.float32)
    ln = lax.broadcasted_iota(jnp.int32, (1, LANES), 1)
    piece = jnp.where(ln % 3 == 0, s0, jnp.where(ln % 3 == 1, s1, s2))
    q_pos = jnp.where(ln < 3, piece * float(POS_RADIX), jnp.where(ln < 6, piece, 0.0))
    q_pos = jnp.broadcast_to(q_pos, (rows, LANES)).astype(jnp.bfloat16)

    qb = q_ref[0]
    lane = lax.broadcasted_iota(jnp.int32, (tq, LANES), 1)
    zero = jnp.zeros_like(qb)
    qs = jnp.concatenate([jnp.where(lane < A_QK_DIM, qb, zero),
                          jnp.where(lane >= A_QK_DIM, qb, zero)], axis=0)
    qs = jnp.concatenate([qs, q_pos], axis=1)

    def scores(kt):
        k0 = pl.multiple_of(kt * tk, tk)
        kk = jnp.concatenate([k_ref[0, pl.ds(k0, tk), :], pk_ref[pl.ds(k0, tk), :]], axis=1)
        return lax.dot_general(qs, kk, _NT, preferred_element_type=jnp.float32)

    def meta_scores():
        kk = jnp.concatenate([km_ref[...], pkm_ref[...]], axis=1)
        s = lax.dot_general(qs, kk, _NT, preferred_element_type=jnp.float32)
        cm = lax.broadcasted_iota(jnp.int32, (rows, META_PAD), 1)
        return jnp.where(cm < N_META, s, -jnp.inf)

    def diag_mask(k0):
        r = lax.broadcasted_iota(jnp.int32, (rows, tk), 0)
        r = jnp.where(r >= tq, r - tq, r)
        c = lax.broadcasted_iota(jnp.int32, (rows, tk), 1)
        return c - r <= q_start - k0

    _causal_flash(q_start, tk, scores, meta_scores, diag_mask,
                  v_ref, vm_ref, sa_ref, sb_ref, m_ref, acc_ref)

    lp = lam_ref[...]
    lam = (jnp.exp(jnp.sum(lp[0:1] * lp[1:2], axis=-1, keepdims=True))
           - jnp.exp(jnp.sum(lp[2:3] * lp[3:4], axis=-1, keepdims=True)) + lam_init)
    o = acc_ref[:, 0:A_V_DIM] / acc_ref[:, A_V_DIM:A_V_DIM + LANES]
    o = o[0:tq] - lam * o[tq:rows]
    o = (o * lax.rsqrt(jnp.mean(o * o, axis=-1, keepdims=True) + RMS_EPS)) * sw_ref[...]
    o = o * (1.0 - lam_init)
    o_ref[0] = (o * _silu(g_ref[0])).astype(o_ref.dtype)


def _pos_columns(pos):
    a = (pos // POS_RADIX).astype(jnp.float32)[:, None]
    b = (pos % POS_RADIX).astype(jnp.float32)[:, None]
    ln = jnp.arange(LANES)[None, :]
    return jnp.where(ln < 3, a, jnp.where(ln < 6, b, 0.0)).astype(jnp.bfloat16)


def _diff_attn(slopes, qkv, kv_meta, gates, lam_p, subln_w, tq, tk, lam_init):
    b, s, _ = qkv.shape
    hq = A_HEADS
    pos_x = _pos_columns(jnp.arange(s, dtype=jnp.int32) + N_META)
    pos_m = _pos_columns(jnp.arange(META_PAD, dtype=jnp.int32))
    return pl.pallas_call(
        functools.partial(_diff_attn_kernel, tq=tq, tk=tk, lam_init=lam_init),
        grid=(b, hq, s // tq),
        in_specs=[
            pl.BlockSpec(memory_space=pltpu.SMEM),
            pl.BlockSpec((1, tq, LANES), lambda b, h, t: (b, t, h)),
            pl.BlockSpec((1, s, LANES), lambda b, h, t: (b, 0, hq + h)),
            pl.BlockSpec((1, s, LANES), lambda b, h, t: (b, 0, 2 * hq + h)),
            pl.BlockSpec((s, LANES), lambda b, h, t: (0, 0)),
            pl.BlockSpec((META_PAD, LANES), lambda b, h, t: (0, hq + h)),
            pl.BlockSpec((META_PAD, LANES), lambda b, h, t: (0, 2 * hq + h)),
            pl.BlockSpec((META_PAD, LANES), lambda b, h, t: (0, 0)),
            pl.BlockSpec((1, tq, LANES), lambda b, h, t: (b, t, h)),
            pl.BlockSpec(lam_p.shape, lambda b, h, t: (0, 0)),
            pl.BlockSpec(subln_w.shape, lambda b, h, t: (0, 0)),
        ],
        out_specs=pl.BlockSpec((1, tq, LANES), lambda b, h, t: (b, t, h)),
        out_shape=jax.ShapeDtypeStruct((b, s, A_WIDTH), jnp.bfloat16),
        scratch_shapes=[pltpu.VMEM((2 * tq, tk), jnp.float32),
                        pltpu.VMEM((2 * tq, tk), jnp.float32),
                        pltpu.VMEM((2 * tq, LANES), jnp.float32),
                        pltpu.VMEM((2 * tq, A_V_DIM + LANES), jnp.float32)],
        compiler_params=_cparams(3),
        name="diff_attn",
    )(slopes, qkv, qkv, qkv, pos_x, kv_meta, kv_meta, pos_m, gates, lam_p, subln_w)


def _mla_attn_kernel(q_ref, k_ref, v_ref, km_ref, vm_ref, g_ref, o_ref,
                     sa_ref, sb_ref, m_ref, acc_ref, *, tq, tk):
    t = pl.program_id(2)
    q_start = t * tq
    q = q_ref[0]

    def scores(kt):
        k0 = pl.multiple_of(kt * tk, tk)
        return lax.dot_general(q, k_ref[0, pl.ds(k0, tk), :], _NT,
                               preferred_element_type=jnp.float32)

    def meta_scores():
        s = lax.dot_general(q, km_ref[...], _NT, preferred_element_type=jnp.float32)
        cm = lax.broadcasted_iota(jnp.int32, (tq, META_PAD), 1)
        return jnp.where(cm < N_META, s, -jnp.inf)

    def diag_mask(k0):
        r = lax.broadcasted_iota(jnp.int32, (tq, tk), 0)
        c = lax.broadcasted_iota(jnp.int32, (tq, tk), 1)
        return c - r <= q_start - k0

    _causal_flash(q_start, tk, scores, meta_scores, diag_mask,
                  v_ref, vm_ref, sa_ref, sb_ref, m_ref, acc_ref)

    o = acc_ref[:, 0:B_V_DIM] / acc_ref[:, B_V_DIM:B_V_DIM + LANES]
    o_ref[0] = (o * _silu(g_ref[0])).astype(o_ref.dtype)


def _mla_attn(q_cat, k_cat, v, km, vm, gates, tq, tk):
    b, s, _ = q_cat.shape
    g_blk0 = A_WIDTH // LANES
    return pl.pallas_call(
        functools.partial(_mla_attn_kernel, tq=tq, tk=tk),
        grid=(b, B_HEADS, s // tq),
        in_specs=[
            pl.BlockSpec((1, tq, B_QK_PAD), lambda b, h, t: (b, t, h)),
            pl.BlockSpec((1, s, B_QK_PAD), lambda b, h, t: (b, 0, h)),
            pl.BlockSpec((1, s, B_V_DIM), lambda b, h, t: (b, 0, h)),
            pl.BlockSpec((META_PAD, B_QK_PAD), lambda b, h, t: (0, h)),
            pl.BlockSpec((META_PAD, B_V_DIM), lambda b, h, t: (0, h)),
            pl.BlockSpec((1, tq, LANES), lambda b, h, t: (b, t, g_blk0 + h)),
        ],
        out_specs=pl.BlockSpec((1, tq, B_V_DIM), lambda b, h, t: (b, t, h)),
        out_shape=jax.ShapeDtypeStruct((b, s, B_WIDTH), jnp.bfloat16),
        scratch_shapes=[pltpu.VMEM((tq, tk), jnp.float32),
                        pltpu.VMEM((tq, tk), jnp.float32),
                        pltpu.VMEM((tq, LANES), jnp.float32),
                        pltpu.VMEM((tq, B_V_DIM + LANES), jnp.float32)],
        compiler_params=_cparams(3),
        name="mla_attn",
    )(q_cat, k_cat, v, km, vm, gates)


def _out_proj_kernel(oa_ref, ob_ref, x_ref, wa_ref, wb_ref, fw_ref, y_ref):
    d = jnp.dot(oa_ref[...], wa_ref[...], preferred_element_type=jnp.float32)
    d = d + jnp.dot(ob_ref[...], wb_ref[...], preferred_element_type=jnp.float32)
    hres = x_ref[...] + d
    r = lax.rsqrt(jnp.mean(hres * hres, axis=-1, keepdims=True) + RMS_EPS)
    y_ref[...] = (hres * r) * fw_ref[...]


def _out_proj(oa, ob, x, wa, wb, fw, tm):
    rows, d = x.shape
    full = lambda a: pl.BlockSpec(a.shape, lambda i: (0, 0))
    return pl.pallas_call(
        _out_proj_kernel,
        grid=(rows // tm,),
        in_specs=[pl.BlockSpec((tm, oa.shape[1]), lambda i: (i, 0)),
                  pl.BlockSpec((tm, ob.shape[1]), lambda i: (i, 0)),
                  pl.BlockSpec((tm, d), lambda i: (i, 0)),
                  full(wa), full(wb), full(fw)],
        out_specs=pl.BlockSpec((tm, d), lambda i: (i, 0)),
        out_shape=jax.ShapeDtypeStruct((rows, d), jnp.float32),
        compiler_params=_cparams(1),
        name="out_proj",
    )(oa, ob, x, wa, wb, fw)


def _rot_half_cols(w):
    half = w.shape[-1] // 2
    return jnp.concatenate([-w[..., half:], w[..., :half]], axis=-1)


def _pad_cols(w, width):
    return jnp.pad(w, ((0, 0), (0, width - w.shape[1])))


def kernel(x, meta_tokens, attn_norm_w, w_in, diff_lambda, diff_subln_w, mla_q_norm_w, w_uq,
           mla_kv_norm_w, w_ukv, w_out, final_norm_w):
    bsz, seq, d = x.shape
    bf16 = jnp.bfloat16
    l = 0
    lam_init = 0.8 - 0.6 * math.exp(-0.3 * l)

    wi = w_in[l]
    a_q, a_k, a_v, a_g, b_cq, b_ckv, b_kr, b_g = jnp.split(
        wi, [1024, 2048, 3072, 4096, 4608, 4864, 4928], axis=1)
    w_qkv = jnp.concatenate([a_q * (A_QK_DIM ** -0.5 * LOG2E), a_k, a_v], axis=1).astype(bf16)
    w_gates = jnp.concatenate([a_g, b_g], axis=1).astype(bf16)
    w_small = jnp.concatenate([b_cq, b_ckv, _pad_cols(b_kr, LANES),
                               _pad_cols(_rot_half_cols(b_kr), LANES)], axis=1).astype(bf16)

    wuq = w_uq[l].reshape(B_Q_LORA, B_HEADS, B_NOPE + B_ROPE)
    wq_main = jnp.pad(wuq, ((0, 0), (0, 0), (0, B_QK_PAD - B_NOPE - B_ROPE)))
    wq_main = wq_main.reshape(B_Q_LORA, B_HEADS * B_QK_PAD).astype(bf16)
    wq_rot = jnp.pad(_rot_half_cols(wuq[..., B_NOPE:]), ((0, 0), (0, 0), (0, LANES - B_ROPE)))
    wq_rot = wq_rot.reshape(B_Q_LORA, B_HEADS * LANES).astype(bf16)
    wukv = w_ukv[l].reshape(B_KV_LORA, B_HEADS, B_NOPE + B_V_DIM)
    wk = wukv[..., :B_NOPE].reshape(B_KV_LORA, B_HEADS * B_NOPE).astype(bf16)
    wv = wukv[..., B_NOPE:].reshape(B_KV_LORA, B_WIDTH).astype(bf16)
    wo = w_out[l].astype(bf16)

    half = B_ROPE // 2
    inv = ROPE_THETA ** (-jnp.arange(half, dtype=jnp.float32) / half)
    inv = jnp.tile(inv, LANES // half)[None, :]
    slopes = jnp.exp2(-8.0 * jnp.arange(1, A_HEADS + 1, dtype=jnp.float32) / A_HEADS)

    nw = attn_norm_w[l][None, :]
    qnw = mla_q_norm_w[l][None, :]
    kvnw = mla_kv_norm_w[l][None, :]

    x2 = x.reshape(bsz * seq, d)
    qkv = _norm_matmul(x2, nw, w_qkv, bf16, 512, 1024)
    gates = _norm_matmul(x2, nw, w_gates, jnp.float32, 512, 1024)
    small = _norm_matmul(x2, nw, w_small, jnp.float32, 512, 1024)
    meta = meta_tokens.astype(x.dtype)
    qkv_m = _norm_matmul(meta, nw, w_qkv, bf16, N_META, 1024)
    small_m = _norm_matmul(meta, nw, w_small, jnp.float32, N_META, 1024)

    q_cat, k_cat, v_b = _mla_proj(small, qnw, kvnw, inv, wq_main, wq_rot, wk, wv, 512, seq, N_META)
    _, k_cat_m, v_b_m = _mla_proj(small_m, qnw, kvnw, inv, wq_main, wq_rot, wk, wv, N_META,
                                  N_META, 0)

    pad_meta = lambda a: jnp.pad(a, ((0, META_PAD - N_META), (0, 0)))

    o_a = _diff_attn(slopes, qkv.reshape(bsz, seq, -1), pad_meta(qkv_m),
                     gates.reshape(bsz, seq, -1), diff_lambda[l].astype(jnp.float32),
                     diff_subln_w[l][None, :], 256, 512, lam_init)
    o_b = _mla_attn(q_cat.reshape(bsz, seq, -1), k_cat.reshape(bsz, seq, -1),
                    v_b.reshape(bsz, seq, -1), pad_meta(k_cat_m), pad_meta(v_b_m),
                    gates.reshape(bsz, seq, -1), 512, 512)

    y = _out_proj(o_a.reshape(bsz * seq, -1), o_b.reshape(bsz * seq, -1), x2,
                  wo[:A_WIDTH], wo[A_WIDTH:], final_norm_w[None, :], 256)
    return y.reshape(bsz, seq, d)
```

```python
import functools
import math

import jax
import jax.numpy as jnp
from jax import lax
from jax.experimental import pallas as pl
from jax.experimental.pallas import tpu as pltpu

N_META = 16
RMS_EPS = 1e-6
ROPE_THETA = 10000.0
LOG2E = math.log2(math.e)

A_HEADS = 8
A_QK_DIM = 64
A_V_DIM = 128
A_WIDTH = A_HEADS * A_V_DIM
B_HEADS = 8
B_Q_LORA = 512
B_KV_LORA = 256
B_NOPE = 128
B_ROPE = 64
B_V_DIM = 128
B_WIDTH = B_HEADS * B_V_DIM
B_QK_PAD = 256

LANES = 128
META_PAD = 128
POS_RADIX = 256
VMEM_LIMIT = 48 * 1024 * 1024

_NT = (((1,), (1,)), ((), ()))


def _cparams(n_grid):
    return pltpu.CompilerParams(
        dimension_semantics=("arbitrary",) * n_grid, vmem_limit_bytes=VMEM_LIMIT)


def _norm_matmul_kernel(x_ref, nw_ref, w_ref, o_ref, xn_ref):
    @pl.when(pl.program_id(1) == 0)
    def _():
        x = x_ref[...]
        r = lax.rsqrt(jnp.mean(x * x, axis=-1, keepdims=True) + RMS_EPS)
        xn_ref[...] = ((x * r) * nw_ref[...]).astype(xn_ref.dtype)

    o_ref[...] = jnp.dot(xn_ref[...], w_ref[...],
                         preferred_element_type=jnp.float32).astype(o_ref.dtype)


def _norm_matmul(x, nw, w, out_dtype, tm, tn):
    rows, k = x.shape
    n = w.shape[1]
    return pl.pallas_call(
        _norm_matmul_kernel,
        grid=(rows // tm, n // tn),
        in_specs=[
            pl.BlockSpec((tm, k), lambda i, j: (i, 0)),
            pl.BlockSpec((1, k), lambda i, j: (0, 0)),
            pl.BlockSpec((k, tn), lambda i, j: (0, j)),
        ],
        out_specs=pl.BlockSpec((tm, tn), lambda i, j: (i, j)),
        out_shape=jax.ShapeDtypeStruct((rows, n), out_dtype),
        scratch_shapes=[pltpu.VMEM((tm, k), jnp.bfloat16)],
        compiler_params=_cparams(2),
        name="norm_matmul",
    )(x, nw, w)


def _mla_proj_kernel(small_ref, qnw_ref, kvnw_ref, inv_ref, wq_ref, wqr_ref, wk_ref, wv_ref,
                     q_ref, k_ref, v_ref, *, tm, seq, pos0, scale):
    def rms(x, w):
        r = lax.rsqrt(jnp.mean(x * x, axis=-1, keepdims=True) + RMS_EPS)
        return (x * r) * w

    cq = rms(small_ref[:, 0:B_Q_LORA], qnw_ref[...]).astype(jnp.bfloat16)
    ckv = rms(small_ref[:, B_Q_LORA:B_Q_LORA + B_KV_LORA], kvnw_ref[...]).astype(jnp.bfloat16)
    kr = small_ref[:, 768:896]
    kr_rot = small_ref[:, 896:1024]

    row = lax.broadcasted_iota(jnp.int32, (tm, LANES), 0)
    pos = (row + ((pl.program_id(0) * tm) % seq + pos0)).astype(jnp.float32)
    ang = pos * inv_ref[...]
    cos, sin = jnp.cos(ang), jnp.sin(ang)

    q1 = jnp.dot(cq, wq_ref[...], preferred_element_type=jnp.float32)
    q2 = jnp.dot(cq, wqr_ref[...], preferred_element_type=jnp.float32)
    kn = jnp.dot(ckv, wk_ref[...], preferred_element_type=jnp.float32)
    v_ref[...] = jnp.dot(ckv, wv_ref[...], preferred_element_type=jnp.float32).astype(v_ref.dtype)
    kpe = (kr * cos + kr_rot * sin).astype(k_ref.dtype)
    for h in range(B_HEADS):
        c0 = h * B_QK_PAD
        q_ref[:, c0:c0 + LANES] = (q1[:, c0:c0 + LANES] * scale).astype(q_ref.dtype)
        qpe = q1[:, c0 + LANES:c0 + 2 * LANES] * cos + q2[:, h * LANES:(h + 1) * LANES] * sin
        q_ref[:, c0 + LANES:c0 + 2 * LANES] = (qpe * scale).astype(q_ref.dtype)
        k_ref[:, c0:c0 + LANES] = kn[:, h * LANES:(h + 1) * LANES].astype(k_ref.dtype)
        k_ref[:, c0 + LANES:c0 + 2 * LANES] = kpe


def _mla_proj(small, qnw, kvnw, inv, wq, wqr, wk, wv, tm, seq, pos0):
    rows = small.shape[0]
    full = lambda a: pl.BlockSpec(a.shape, lambda i: (0, 0))
    scale = (B_NOPE + B_ROPE) ** -0.5 * LOG2E
    return pl.pallas_call(
        functools.partial(_mla_proj_kernel, tm=tm, seq=seq, pos0=pos0, scale=scale),
        grid=(rows // tm,),
        in_specs=[pl.BlockSpec((tm, small.shape[1]), lambda i: (i, 0)),
                  full(qnw), full(kvnw), full(inv), full(wq), full(wqr), full(wk), full(wv)],
        out_specs=[pl.BlockSpec((tm, B_HEADS * B_QK_PAD), lambda i: (i, 0)),
                   pl.BlockSpec((tm, B_HEADS * B_QK_PAD), lambda i: (i, 0)),
                   pl.BlockSpec((tm, B_WIDTH), lambda i: (i, 0))],
        out_shape=[jax.ShapeDtypeStruct((rows, B_HEADS * B_QK_PAD), jnp.bfloat16),
                   jax.ShapeDtypeStruct((rows, B_HEADS * B_QK_PAD), jnp.bfloat16),
                   jax.ShapeDtypeStruct((rows, B_WIDTH), jnp.bfloat16)],
        compiler_params=_cparams(1),
        name="mla_proj",
    )(small, qnw, kvnw, inv, wq, wqr, wk, wv)


def _silu(g):
    return g * (1.0 / (1.0 + jnp.exp(-g)))


def _lanes(x, width):
    return jnp.tile(x, (1, width // LANES))


def _with_ones(v):
    return jnp.concatenate([v, jnp.ones((v.shape[0], LANES), v.dtype)], axis=1)


def _online_step(s, v, m_ref, acc_ref):
    tk = s.shape[-1]
    m_prev = m_ref[...]
    m_new = jnp.maximum(m_prev, jnp.max(s, axis=-1, keepdims=True))
    alpha = jnp.exp2(m_prev - m_new)
    p = jnp.exp2(s - _lanes(m_new, tk))
    pv = jnp.dot(p.astype(v.dtype), _with_ones(v), preferred_element_type=jnp.float32)
    acc_ref[...] = _lanes(alpha, acc_ref.shape[-1]) * acc_ref[...] + pv
    m_ref[...] = m_new


def _first_step(s, v, m_ref, acc_ref):
    tk = s.shape[-1]
    m_new = jnp.broadcast_to(jnp.max(s, axis=-1, keepdims=True), m_ref.shape)
    p = jnp.exp2(s - _lanes(m_new, tk))
    acc_ref[...] = jnp.dot(p.astype(v.dtype), _with_ones(v), preferred_element_type=jnp.float32)
    m_ref[...] = m_new


def _causal_flash(q_start, tk, scores, meta_scores, diag_mask,
                  v_ref, vm_ref, sa_ref, sb_ref, m_ref, acc_ref):
    n_full = q_start // tk

    def softmax_pv(s_ref, kt, masked):
        k0 = pl.multiple_of(kt * tk, tk)
        s = s_ref[...]
        if masked:
            s = jnp.where(diag_mask(k0), s, -jnp.inf)
        _online_step(s, v_ref[0, pl.ds(k0, tk), :], m_ref, acc_ref)

    sa_ref[...] = scores(0)
    _first_step(meta_scores(), vm_ref[...], m_ref, acc_ref)

    def pair(j, carry):
        sb_ref[...] = scores(2 * j + 1)
        softmax_pv(sa_ref, 2 * j, False)
        sa_ref[...] = scores(2 * j + 2)
        softmax_pv(sb_ref, 2 * j + 1, False)
        return carry

    n_pairs = n_full // 2
    lax.fori_loop(0, n_pairs, pair, 0)
    kt = 2 * n_pairs

    @pl.when(n_full > kt)
    def _():
        sb_ref[...] = scores(kt + 1)
        softmax_pv(sa_ref, kt, False)
        softmax_pv(sb_ref, kt + 1, True)

    @pl.when(n_full == kt)
    def _():
        softmax_pv(sa_ref, kt, True)


def _diff_attn_kernel(slopes_ref, q_ref, k_ref, v_ref, pk_ref, km_ref, vm_ref, pkm_ref, g_ref,
                      lam_ref, sw_ref, o_ref, sa_ref, sb_ref, m_ref, acc_ref,
                      *, tq, tk, lam_init):
    h = pl.program_id(1)
    t = pl.program_id(2)
    rows = 2 * tq
    q_start = t * tq

    slope = jnp.full((1, LANES), slopes_ref[h] * LOG2E, jnp.float32)
    s0 = slope.astype(jnp.bfloat16).astype(jnp.float32)
    s1 = (slope - s0).astype(jnp.bfloat16).astype(jnp.float32)
    s2 = (slope - s0 - s1).astype(jnp.bfloat16).astype(jnp.float32)
    ln = lax.broadcasted_iota(jnp.int32, (1, LANES), 1)
    piece = jnp.where(ln % 3 == 0, s0, jnp.where(ln % 3 == 1, s1, s2))
    q_pos = jnp.where(ln < 3, piece * float(POS_RADIX), jnp.where(ln < 6, piece, 0.0))
    q_pos = jnp.broadcast_to(q_pos, (rows, LANES)).astype(jnp.bfloat16)

    qb = q_ref[0]
    lane = lax.broadcasted_iota(jnp.int32, (tq, LANES), 1)
    zero = jnp.zeros_like(qb)
    qs = jnp.concatenate([jnp.where(lane < A_QK_DIM, qb, zero),
                          jnp.where(lane >= A_QK_DIM, qb, zero)], axis=0)
    qs = jnp.concatenate([qs, q_pos], axis=1)

    def scores(kt):
        k0 = pl.multiple_of(kt * tk, tk)
        kk = jnp.concatenate([k_ref[0, pl.ds(k0, tk), :], pk_ref[pl.ds(k0, tk), :]], axis=1)
        return lax.dot_general(qs, kk, _NT, preferred_element_type=jnp.float32)

    def meta_scores():
        kk = jnp.concatenate([km_ref[...], pkm_ref[...]], axis=1)
        s = lax.dot_general(qs, kk, _NT, preferred_element_type=jnp.float32)
        cm = lax.broadcasted_iota(jnp.int32, (rows, META_PAD), 1)
        return jnp.where(cm < N_META, s, -jnp.inf)

    def diag_mask(k0):
        r = lax.broadcasted_iota(jnp.int32, (rows, tk), 0)
        r = jnp.where(r >= tq, r - tq, r)
        c = lax.broadcasted_iota(jnp.int32, (rows, tk), 1)
        return c - r <= q_start - k0

    _causal_flash(q_start, tk, scores, meta_scores, diag_mask,
                  v_ref, vm_ref, sa_ref, sb_ref, m_ref, acc_ref)

    lp = lam_ref[...]
    lam = (jnp.exp(jnp.sum(lp[0:1] * lp[1:2], axis=-1, keepdims=True))
           - jnp.exp(jnp.sum(lp[2:3] * lp[3:4], axis=-1, keepdims=True)) + lam_init)
    o = acc_ref[:, 0:A_V_DIM] / acc_ref[:, A_V_DIM:A_V_DIM + LANES]
    o = o[0:tq] - lam * o[tq:rows]
    o = (o * lax.rsqrt(jnp.mean(o * o, axis=-1, keepdims=True) + RMS_EPS)) * sw_ref[...]
    o = o * (1.0 - lam_init)
    o_ref[0] = (o * _silu(g_ref[0])).astype(o_ref.dtype)


def _pos_columns(pos):
    a = (pos // POS_RADIX).astype(jnp.float32)[:, None]
    b = (pos % POS_RADIX).astype(jnp.float32)[:, None]
    ln = jnp.arange(LANES)[None, :]
    return jnp.where(ln < 3, a, jnp.where(ln < 6, b, 0.0)).astype(jnp.bfloat16)


def _diff_attn(slopes, qkv, kv_meta, gates, lam_p, subln_w, tq, tk, lam_init):
    b, s, _ = qkv.shape
    hq = A_HEADS
    pos_x = _pos_columns(jnp.arange(s, dtype=jnp.int32) + N_META)
    pos_m = _pos_columns(jnp.arange(META_PAD, dtype=jnp.int32))
    return pl.pallas_call(
        functools.partial(_diff_attn_kernel, tq=tq, tk=tk, lam_init=lam_init),
        grid=(b, hq, s // tq),
        in_specs=[
            pl.BlockSpec(memory_space=pltpu.SMEM),
            pl.BlockSpec((1, tq, LANES), lambda b, h, t: (b, t, h)),
            pl.BlockSpec((1, s, LANES), lambda b, h, t: (b, 0, hq + h)),
            pl.BlockSpec((1, s, LANES), lambda b, h, t: (b, 0, 2 * hq + h)),
            pl.BlockSpec((s, LANES), lambda b, h, t: (0, 0)),
            pl.BlockSpec((META_PAD, LANES), lambda b, h, t: (0, hq + h)),
            pl.BlockSpec((META_PAD, LANES), lambda b, h, t: (0, 2 * hq + h)),
            pl.BlockSpec((META_PAD, LANES), lambda b, h, t: (0, 0)),
            pl.BlockSpec((1, tq, LANES), lambda b, h, t: (b, t, h)),
            pl.BlockSpec(lam_p.shape, lambda b, h, t: (0, 0)),
            pl.BlockSpec(subln_w.shape, lambda b, h, t: (0, 0)),
        ],
        out_specs=pl.BlockSpec((1, tq, LANES), lambda b, h, t: (b, t, h)),
        out_shape=jax.ShapeDtypeStruct((b, s, A_WIDTH), jnp.bfloat16),
        scratch_shapes=[pltpu.VMEM((2 * tq, tk), jnp.float32),
                        pltpu.VMEM((2 * tq, tk), jnp.float32),
                        pltpu.VMEM((2 * tq, LANES), jnp.float32),
                        pltpu.VMEM((2 * tq, A_V_DIM + LANES), jnp.float32)],
        compiler_params=_cparams(3),
        name="diff_attn",
    )(slopes, qkv, qkv, qkv, pos_x, kv_meta, kv_meta, pos_m, gates, lam_p, subln_w)


def _mla_attn_kernel(q_ref, k_ref, v_ref, km_ref, vm_ref, g_ref, o_ref,
                     sa_ref, sb_ref, m_ref, acc_ref, *, tq, tk):
    t = pl.program_id(2)
    q_start = t * tq
    q = q_ref[0]

    def scores(kt):
        k0 = pl.multiple_of(kt * tk, tk)
        return lax.dot_general(q, k_ref[0, pl.ds(k0, tk), :], _NT,
                               preferred_element_type=jnp.float32)

    def meta_scores():
        s = lax.dot_general(q, km_ref[...], _NT, preferred_element_type=jnp.float32)
        cm = lax.broadcasted_iota(jnp.int32, (tq, META_PAD), 1)
        return jnp.where(cm < N_META, s, -jnp.inf)

    def diag_mask(k0):
        r = lax.broadcasted_iota(jnp.int32, (tq, tk), 0)
        c = lax.broadcasted_iota(jnp.int32, (tq, tk), 1)
        return c - r <= q_start - k0

    _causal_flash(q_start, tk, scores, meta_scores, diag_mask,
                  v_ref, vm_ref, sa_ref, sb_ref, m_ref, acc_ref)

    o = acc_ref[:, 0:B_V_DIM] / acc_ref[:, B_V_DIM:B_V_DIM + LANES]
    o_ref[0] = (o * _silu(g_ref[0])).astype(o_ref.dtype)


def _mla_attn(q_cat, k_cat, v, km, vm, gates, tq, tk):
    b, s, _ = q_cat.shape
    g_blk0 = A_WIDTH // LANES
    return pl.pallas_call(
        functools.partial(_mla_attn_kernel, tq=tq, tk=tk),
        grid=(b, B_HEADS, s // tq),
        in_specs=[
            pl.BlockSpec((1, tq, B_QK_PAD), lambda b, h, t: (b, t, h)),
            pl.BlockSpec((1, s, B_QK_PAD), lambda b, h, t: (b, 0, h)),
            pl.BlockSpec((1, s, B_V_DIM), lambda b, h, t: (b, 0, h)),
            pl.BlockSpec((META_PAD, B_QK_PAD), lambda b, h, t: (0, h)),
            pl.BlockSpec((META_PAD, B_V_DIM), lambda b, h, t: (0, h)),
            pl.BlockSpec((1, tq, LANES), lambda b, h, t: (b, t, g_blk0 + h)),
        ],
        out_specs=pl.BlockSpec((1, tq, B_V_DIM), lambda b, h, t: (b, t, h)),
        out_shape=jax.ShapeDtypeStruct((b, s, B_WIDTH), jnp.bfloat16),
        scratch_shapes=[pltpu.VMEM((tq, tk), jnp.float32),
                        pltpu.VMEM((tq, tk), jnp.float32),
                        pltpu.VMEM((tq, LANES), jnp.float32),
                        pltpu.VMEM((tq, B_V_DIM + LANES), jnp.float32)],
        compiler_params=_cparams(3),
        name="mla_attn",
    )(q_cat, k_cat, v, km, vm, gates)


def _out_proj_kernel(oa_ref, ob_ref, x_ref, wa_ref, wb_ref, fw_ref, y_ref):
    d = jnp.dot(oa_ref[...], wa_ref[...], preferred_element_type=jnp.float32)
    d = d + jnp.dot(ob_ref[...], wb_ref[...], preferred_element_type=jnp.float32)
    hres = x_ref[...] + d
    r = lax.rsqrt(jnp.mean(hres * hres, axis=-1, keepdims=True) + RMS_EPS)
    y_ref[...] = (hres * r) * fw_ref[...]


def _out_proj(oa, ob, x, wa, wb, fw, tm):
    rows, d = x.shape
    full = lambda a: pl.BlockSpec(a.shape, lambda i: (0, 0))
    return pl.pallas_call(
        _out_proj_kernel,
        grid=(rows // tm,),
        in_specs=[pl.BlockSpec((tm, oa.shape[1]), lambda i: (i, 0)),
                  pl.BlockSpec((tm, ob.shape[1]), lambda i: (i, 0)),
                  pl.BlockSpec((tm, d), lambda i: (i, 0)),
                  full(wa), full(wb), full(fw)],
        out_specs=pl.BlockSpec((tm, d), lambda i: (i, 0)),
        out_shape=jax.ShapeDtypeStruct((rows, d), jnp.float32),
        compiler_params=_cparams(1),
        name="out_proj",
    )(oa, ob, x, wa, wb, fw)


def _rot_half_cols(w):
    half = w.shape[-1] // 2
    return jnp.concatenate([-w[..., half:], w[..., :half]], axis=-1)


def _pad_cols(w, width):
    return jnp.pad(w, ((0, 0), (0, width - w.shape[1])))


def kernel(x, meta_tokens, attn_norm_w, w_in, diff_lambda, diff_subln_w, mla_q_norm_w, w_uq,
           mla_kv_norm_w, w_ukv, w_out, final_norm_w):
    bsz, seq, d = x.shape
    bf16 = jnp.bfloat16
    l = 0
    lam_init = 0.8 - 0.6 * math.exp(-0.3 * l)

    wi = w_in[l]
    a_q, a_k, a_v, a_g, b_cq, b_ckv, b_kr, b_g = jnp.split(
        wi, [1024, 2048, 3072, 4096, 4608, 4864, 4928], axis=1)
    w_qkv = jnp.concatenate([a_q * (A_QK_DIM ** -0.5 * LOG2E), a_k, a_v], axis=1).astype(bf16)
    w_gates = jnp.concatenate([a_g, b_g], axis=1).astype(bf16)
    w_small = jnp.concatenate([b_cq, b_ckv, _pad_cols(b_kr, LANES),
                               _pad_cols(_rot_half_cols(b_kr), LANES)], axis=1).astype(bf16)

    wuq = w_uq[l].reshape(B_Q_LORA, B_HEADS, B_NOPE + B_ROPE)
    wq_main = jnp.pad(wuq, ((0, 0), (0, 0), (0, B_QK_PAD - B_NOPE - B_ROPE)))
    wq_main = wq_main.reshape(B_Q_LORA, B_HEADS * B_QK_PAD).astype(bf16)
    wq_rot = jnp.pad(_rot_half_cols(wuq[..., B_NOPE:]), ((0, 0), (0, 0), (0, LANES - B_ROPE)))
    wq_rot = wq_rot.reshape(B_Q_LORA, B_HEADS * LANES).astype(bf16)
    wukv = w_ukv[l].reshape(B_KV_LORA, B_HEADS, B_NOPE + B_V_DIM)
    wk = wukv[..., :B_NOPE].reshape(B_KV_LORA, B_HEADS * B_NOPE).astype(bf16)
    wv = wukv[..., B_NOPE:].reshape(B_KV_LORA, B_WIDTH).astype(bf16)
    wo = w_out[l].astype(bf16)

    half = B_ROPE // 2
    inv = ROPE_THETA ** (-jnp.arange(half, dtype=jnp.float32) / half)
    inv = jnp.tile(inv, LANES // half)[None, :]
    slopes = jnp.exp2(-8.0 * jnp.arange(1, A_HEADS + 1, dtype=jnp.float32) / A_HEADS)

    nw = attn_norm_w[l][None, :]
    qnw = mla_q_norm_w[l][None, :]
    kvnw = mla_kv_norm_w[l][None, :]

    x2 = x.reshape(bsz * seq, d)
    qkv = _norm_matmul(x2, nw, w_qkv, bf16, 512, 1024)
    gates = _norm_matmul(x2, nw, w_gates, jnp.float32, 512, 1024)
    small = _norm_matmul(x2, nw, w_small, jnp.float32, 512, 1024)
    meta = meta_tokens.astype(x.dtype)
    qkv_m = _norm_matmul(meta, nw, w_qkv, bf16, N_META, 1024)
    small_m = _norm_matmul(meta, nw, w_small, jnp.float32, N_META, 1024)

    q_cat, k_cat, v_b = _mla_proj(small, qnw, kvnw, inv, wq_main, wq_rot, wk, wv, 512, seq, N_META)
    _, k_cat_m, v_b_m = _mla_proj(small_m, qnw, kvnw, inv, wq_main, wq_rot, wk, wv, N_META,
                                  N_META, 0)

    pad_meta = lambda a: jnp.pad(a, ((0, META_PAD - N_META), (0, 0)))

    o_a = _diff_attn(slopes, qkv.reshape(bsz, seq, -1), pad_meta(qkv_m),
                     gates.reshape(bsz, seq, -1), diff_lambda[l].astype(jnp.float32),
                     diff_subln_w[l][None, :], 512, 512, lam_init)
    o_b = _mla_attn(q_cat.reshape(bsz, seq, -1), k_cat.reshape(bsz, seq, -1),
                    v_b.reshape(bsz, seq, -1), pad_meta(k_cat_m), pad_meta(v_b_m),
                    gates.reshape(bsz, seq, -1), 1024, 1024)

    y = _out_proj(o_a.reshape(bsz * seq, -1), o_b.reshape(bsz * seq, -1), x2,
                  wo[:A_WIDTH], wo[A_WIDTH:], final_norm_w[None, :], 256)
    return y.reshape(bsz, seq, d)
```

```python
import functools
import math

import jax
import jax.numpy as jnp
from jax import lax
from jax.experimental import pallas as pl
from jax.experimental.pallas import tpu as pltpu

N_META = 16
RMS_EPS = 1e-6
ROPE_THETA = 10000.0
LOG2E = math.log2(math.e)

A_HEADS = 8
A_QK_DIM = 64
A_V_DIM = 128
A_WIDTH = A_HEADS * A_V_DIM
B_HEADS = 8
B_Q_LORA = 512
B_KV_LORA = 256
B_NOPE = 128
B_ROPE = 64
B_V_DIM = 128
B_WIDTH = B_HEADS * B_V_DIM
B_QK_PAD = 256

LANES = 128
META_PAD = 128
POS_RADIX = 256
VMEM_LIMIT = 48 * 1024 * 1024

_NT = (((1,), (1,)), ((), ()))


def _cparams(n_grid):
    return pltpu.CompilerParams(
        dimension_semantics=("arbitrary",) * n_grid, vmem_limit_bytes=VMEM_LIMIT)


def _norm_matmul_kernel(x_ref, nw_ref, w_ref, o_ref, xn_ref):
    @pl.when(pl.program_id(1) == 0)
    def _():
        x = x_ref[...]
        r = lax.rsqrt(jnp.mean(x * x, axis=-1, keepdims=True) + RMS_EPS)
        xn_ref[...] = ((x * r) * nw_ref[...]).astype(xn_ref.dtype)

    o_ref[...] = jnp.dot(xn_ref[...], w_ref[...],
                         preferred_element_type=jnp.float32).astype(o_ref.dtype)


def _norm_matmul(x, nw, w, out_dtype, tm, tn):
    rows, k = x.shape
    n = w.shape[1]
    return pl.pallas_call(
        _norm_matmul_kernel,
        grid=(rows // tm, n // tn),
        in_specs=[
            pl.BlockSpec((tm, k), lambda i, j: (i, 0)),
            pl.BlockSpec((1, k), lambda i, j: (0, 0)),
            pl.BlockSpec((k, tn), lambda i, j: (0, j)),
        ],
        out_specs=pl.BlockSpec((tm, tn), lambda i, j: (i, j)),
        out_shape=jax.ShapeDtypeStruct((rows, n), out_dtype),
        scratch_shapes=[pltpu.VMEM((tm, k), jnp.bfloat16)],
        compiler_params=_cparams(2),
        name="norm_matmul",
    )(x, nw, w)


def _mla_proj_kernel(small_ref, qnw_ref, kvnw_ref, inv_ref, wq_ref, wqr_ref, wk_ref, wv_ref,
                     q_ref, k_ref, v_ref, *, tm, seq, pos0, scale):
    def rms(x, w):
        r = lax.rsqrt(jnp.mean(x * x, axis=-1, keepdims=True) + RMS_EPS)
        return (x * r) * w

    cq = rms(small_ref[:, 0:B_Q_LORA], qnw_ref[...]).astype(jnp.bfloat16)
    ckv = rms(small_ref[:, B_Q_LORA:B_Q_LORA + B_KV_LORA], kvnw_ref[...]).astype(jnp.bfloat16)
    kr = small_ref[:, 768:896]
    kr_rot = small_ref[:, 896:1024]

    row = lax.broadcasted_iota(jnp.int32, (tm, LANES), 0)
    pos = (row + ((pl.program_id(0) * tm) % seq + pos0)).astype(jnp.float32)
    ang = pos * inv_ref[...]
    cos, sin = jnp.cos(ang), jnp.sin(ang)

    q1 = jnp.dot(cq, wq_ref[...], preferred_element_type=jnp.float32)
    q2 = jnp.dot(cq, wqr_ref[...], preferred_element_type=jnp.float32)
    kn = jnp.dot(ckv, wk_ref[...], preferred_element_type=jnp.float32)
    v_ref[...] = jnp.dot(ckv, wv_ref[...], preferred_element_type=jnp.float32).astype(v_ref.dtype)
    kpe = (kr * cos + kr_rot * sin).astype(k_ref.dtype)
    for h in range(B_HEADS):
        c0 = h * B_QK_PAD
        q_ref[:, c0:c0 + LANES] = (q1[:, c0:c0 + LANES] * scale).astype(q_ref.dtype)
        qpe = q1[:, c0 + LANES:c0 + 2 * LANES] * cos + q2[:, h * LANES:(h + 1) * LANES] * sin
        q_ref[:, c0 + LANES:c0 + 2 * LANES] = (qpe * scale).astype(q_ref.dtype)
        k_ref[:, c0:c0 + LANES] = kn[:, h * LANES:(h + 1) * LANES].astype(k_ref.dtype)
        k_ref[:, c0 + LANES:c0 + 2 * LANES] = kpe


def _mla_proj(small, qnw, kvnw, inv, wq, wqr, wk, wv, tm, seq, pos0):
    rows = small.shape[0]
    full = lambda a: pl.BlockSpec(a.shape, lambda i: (0, 0))
    scale = (B_NOPE + B_ROPE) ** -0.5 * LOG2E
    return pl.pallas_call(
        functools.partial(_mla_proj_kernel, tm=tm, seq=seq, pos0=pos0, scale=scale),
        grid=(rows // tm,),
        in_specs=[pl.BlockSpec((tm, small.shape[1]), lambda i: (i, 0)),
                  full(qnw), full(kvnw), full(inv), full(wq), full(wqr), full(wk), full(wv)],
        out_specs=[pl.BlockSpec((tm, B_HEADS * B_QK_PAD), lambda i: (i, 0)),
                   pl.BlockSpec((tm, B_HEADS * B_QK_PAD), lambda i: (i, 0)),
                   pl.BlockSpec((tm, B_WIDTH), lambda i: (i, 0))],
        out_shape=[jax.ShapeDtypeStruct((rows, B_HEADS * B_QK_PAD), jnp.bfloat16),
                   jax.ShapeDtypeStruct((rows, B_HEADS * B_QK_PAD), jnp.bfloat16),
                   jax.ShapeDtypeStruct((rows, B_WIDTH), jnp.bfloat16)],
        compiler_params=_cparams(1),
        name="mla_proj",
    )(small, qnw, kvnw, inv, wq, wqr, wk, wv)


def _silu(g):
    return g * (1.0 / (1.0 + jnp.exp(-g)))


def _lanes(x, width):
    return jnp.tile(x, (1, width // LANES))


def _with_ones(v):
    return jnp.concatenate([v, jnp.ones((v.shape[0], LANES), v.dtype)], axis=1)


def _online_step(s, v, m_ref, acc_ref):
    tk = s.shape[-1]
    m_prev = m_ref[...]
    m_new = jnp.maximum(m_prev, jnp.max(s, axis=-1, keepdims=True))
    alpha = jnp.exp2(m_prev - m_new)
    p = jnp.exp2(s - _lanes(m_new, tk))
    pv = jnp.dot(p.astype(v.dtype), _with_ones(v), preferred_element_type=jnp.float32)
    acc_ref[...] = _lanes(alpha, acc_ref.shape[-1]) * acc_ref[...] + pv
    m_ref[...] = m_new


def _first_step(s, v, m_ref, acc_ref):
    tk = s.shape[-1]
    m_new = jnp.broadcast_to(jnp.max(s, axis=-1, keepdims=True), m_ref.shape)
    p = jnp.exp2(s - _lanes(m_new, tk))
    acc_ref[...] = jnp.dot(p.astype(v.dtype), _with_ones(v), preferred_element_type=jnp.float32)
    m_ref[...] = m_new


def _causal_flash(t, tk, scores, next_scores0, meta_scores, diag_mask,
                  v_ref, vm_ref, sa_ref, sb_ref, sc_ref, m_ref, acc_ref):
    def softmax_pv(s_ref, kt, masked):
        k0 = pl.multiple_of(kt * tk, tk)
        s = s_ref[...]
        if masked:
            s = jnp.where(diag_mask(), s, -jnp.inf)
        _online_step(s, v_ref[0, pl.ds(k0, tk), :], m_ref, acc_ref)

    @pl.when(t == 0)
    def _():
        sc_ref[...] = scores(0)
        _first_step(meta_scores(), vm_ref[...], m_ref, acc_ref)
        softmax_pv(sc_ref, 0, True)
        sc_ref[...] = next_scores0()

    @pl.when(t > 0)
    def _():
        sb_ref[...] = scores(1)
        _first_step(sc_ref[...], v_ref[0, 0:tk, :], m_ref, acc_ref)
        _online_step(meta_scores(), vm_ref[...], m_ref, acc_ref)

        def pair(j, carry):
            sa_ref[...] = scores(2 * j + 2)
            softmax_pv(sb_ref, 2 * j + 1, False)
            sb_ref[...] = scores(2 * j + 3)
            softmax_pv(sa_ref, 2 * j + 2, False)
            return carry

        n_pairs = (t - 1) // 2
        lax.fori_loop(0, n_pairs, pair, 0)
        kt = 2 * n_pairs + 1

        @pl.when(kt == t)
        def _():
            sc_ref[...] = next_scores0()
            softmax_pv(sb_ref, kt, True)

        @pl.when(kt < t)
        def _():
            sa_ref[...] = scores(kt + 1)
            softmax_pv(sb_ref, kt, False)
            sc_ref[...] = next_scores0()
            softmax_pv(sa_ref, kt + 1, True)


def _diff_attn_kernel(slopes_ref, q_ref, qn_ref, k_ref, v_ref, pk_ref, km_ref, vm_ref, pkm_ref,
                      g_ref, lam_ref, sw_ref, o_ref, qs_ref, sa_ref, sb_ref, sc_ref, m_ref,
                      acc_ref, *, tq, tk, lam_init):
    h = pl.program_id(1)
    t = pl.program_id(2)
    rows = 2 * tq

    slope = jnp.full((1, LANES), slopes_ref[h] * LOG2E, jnp.float32)
    s0 = slope.astype(jnp.bfloat16).astype(jnp.float32)
    s1 = (slope - s0).astype(jnp.bfloat16).astype(jnp.float32)
    s2 = (slope - s0 - s1).astype(jnp.bfloat16).astype(jnp.float32)
    ln = lax.broadcasted_iota(jnp.int32, (1, LANES), 1)
    piece = jnp.where(ln % 3 == 0, s0, jnp.where(ln % 3 == 1, s1, s2))
    q_pos = jnp.where(ln < 3, piece * float(POS_RADIX), jnp.where(ln < 6, piece, 0.0))
    q_pos = jnp.broadcast_to(q_pos, (rows, LANES)).astype(jnp.bfloat16)

    def stacked(qb):
        lane = lax.broadcasted_iota(jnp.int32, (tq, LANES), 1)
        zero = jnp.zeros_like(qb)
        qs = jnp.concatenate([jnp.where(lane < A_QK_DIM, qb, zero),
                              jnp.where(lane >= A_QK_DIM, qb, zero)], axis=0)
        return jnp.concatenate([qs, q_pos], axis=1)

    def tile_scores(qs, k0):
        kk = jnp.concatenate([k_ref[0, pl.ds(k0, tk), :], pk_ref[pl.ds(k0, tk), :]], axis=1)
        return lax.dot_general(qs, kk, _NT, preferred_element_type=jnp.float32)

    qs_ref[...] = stacked(q_ref[0])

    def scores(kt):
        return tile_scores(qs_ref[...], pl.multiple_of(kt * tk, tk))

    def next_scores0():
        return tile_scores(stacked(qn_ref[0]), 0)

    def meta_scores():
        kk = jnp.concatenate([km_ref[...], pkm_ref[...]], axis=1)
        s = lax.dot_general(qs_ref[...], kk, _NT, preferred_element_type=jnp.float32)
        cm = lax.broadcasted_iota(jnp.int32, (rows, META_PAD), 1)
        return jnp.where(cm < N_META, s, -jnp.inf)

    def diag_mask():
        r = lax.broadcasted_iota(jnp.int32, (rows, tk), 0)
        r = jnp.where(r >= tq, r - tq, r)
        c = lax.broadcasted_iota(jnp.int32, (rows, tk), 1)
        return c <= r

    _causal_flash(t, tk, scores, next_scores0, meta_scores, diag_mask,
                  v_ref, vm_ref, sa_ref, sb_ref, sc_ref, m_ref, acc_ref)

    lp = lam_ref[...]
    lam = (jnp.exp(jnp.sum(lp[0:1] * lp[1:2], axis=-1, keepdims=True))
           - jnp.exp(jnp.sum(lp[2:3] * lp[3:4], axis=-1, keepdims=True)) + lam_init)
    o = acc_ref[:, 0:A_V_DIM] / acc_ref[:, A_V_DIM:A_V_DIM + LANES]
    o = o[0:tq] - lam * o[tq:rows]
    o = (o * lax.rsqrt(jnp.mean(o * o, axis=-1, keepdims=True) + RMS_EPS)) * sw_ref[...]
    o = o * (1.0 - lam_init)
    o_ref[0] = (o * _silu(g_ref[0])).astype(o_ref.dtype)


def _pos_columns(pos):
    a = (pos // POS_RADIX).astype(jnp.float32)[:, None]
    b = (pos % POS_RADIX).astype(jnp.float32)[:, None]
    ln = jnp.arange(LANES)[None, :]
    return jnp.where(ln < 3, a, jnp.where(ln < 6, b, 0.0)).astype(jnp.bfloat16)


def _diff_attn(slopes, qkv, kv_meta, gates, lam_p, subln_w, tq, tk, lam_init):
    b, s, _ = qkv.shape
    hq = A_HEADS
    pos_x = _pos_columns(jnp.arange(s, dtype=jnp.int32) + N_META)
    pos_m = _pos_columns(jnp.arange(META_PAD, dtype=jnp.int32))
    assert tq == tk
    last = s // tq - 1
    return pl.pallas_call(
        functools.partial(_diff_attn_kernel, tq=tq, tk=tk, lam_init=lam_init),
        grid=(b, hq, s // tq),
        in_specs=[
            pl.BlockSpec(memory_space=pltpu.SMEM),
            pl.BlockSpec((1, tq, LANES), lambda b, h, t: (b, t, h)),
            pl.BlockSpec((1, tq, LANES), lambda b, h, t: (b, jnp.minimum(t + 1, last), h)),
            pl.BlockSpec((1, s, LANES), lambda b, h, t: (b, 0, hq + h)),
            pl.BlockSpec((1, s, LANES), lambda b, h, t: (b, 0, 2 * hq + h)),
            pl.BlockSpec((s, LANES), lambda b, h, t: (0, 0)),
            pl.BlockSpec((META_PAD, LANES), lambda b, h, t: (0, hq + h)),
            pl.BlockSpec((META_PAD, LANES), lambda b, h, t: (0, 2 * hq + h)),
            pl.BlockSpec((META_PAD, LANES), lambda b, h, t: (0, 0)),
            pl.BlockSpec((1, tq, LANES), lambda b, h, t: (b, t, h)),
            pl.BlockSpec(lam_p.shape, lambda b, h, t: (0, 0)),
            pl.BlockSpec(subln_w.shape, lambda b, h, t: (0, 0)),
        ],
        out_specs=pl.BlockSpec((1, tq, LANES), lambda b, h, t: (b, t, h)),
        out_shape=jax.ShapeDtypeStruct((b, s, A_WIDTH), jnp.bfloat16),
        scratch_shapes=[pltpu.VMEM((2 * tq, 2 * LANES), jnp.bfloat16),
                        pltpu.VMEM((2 * tq, tk), jnp.float32),
                        pltpu.VMEM((2 * tq, tk), jnp.float32),
                        pltpu.VMEM((2 * tq, tk), jnp.float32),
                        pltpu.VMEM((2 * tq, LANES), jnp.float32),
                        pltpu.VMEM((2 * tq, A_V_DIM + LANES), jnp.float32)],
        compiler_params=_cparams(3),
        name="diff_attn",
    )(slopes, qkv, qkv, qkv, qkv, pos_x, kv_meta, kv_meta, pos_m, gates, lam_p, subln_w)


def _mla_attn_kernel(q_ref, qn_ref, k_ref, v_ref, km_ref, vm_ref, g_ref, o_ref,
                     sa_ref, sb_ref, sc_ref, m_ref, acc_ref, *, tq, tk):
    t = pl.program_id(2)

    def scores(kt):
        k0 = pl.multiple_of(kt * tk, tk)
        return lax.dot_general(q_ref[0], k_ref[0, pl.ds(k0, tk), :], _NT,
                               preferred_element_type=jnp.float32)

    def next_scores0():
        return lax.dot_general(qn_ref[0], k_ref[0, 0:tk, :], _NT,
                               preferred_element_type=jnp.float32)

    def meta_scores():
        s = lax.dot_general(q_ref[0], km_ref[...], _NT, preferred_element_type=jnp.float32)
        cm = lax.broadcasted_iota(jnp.int32, (tq, META_PAD), 1)
        return jnp.where(cm < N_META, s, -jnp.inf)

    def diag_mask():
        r = lax.broadcasted_iota(jnp.int32, (tq, tk), 0)
        c = lax.broadcasted_iota(jnp.int32, (tq, tk), 1)
        return c <= r

    _causal_flash(t, tk, scores, next_scores0, meta_scores, diag_mask,
                  v_ref, vm_ref, sa_ref, sb_ref, sc_ref, m_ref, acc_ref)

    o = acc_ref[:, 0:B_V_DIM] / acc_ref[:, B_V_DIM:B_V_DIM + LANES]
    o_ref[0] = (o * _silu(g_ref[0])).astype(o_ref.dtype)


def _mla_attn(q_cat, k_cat, v, km, vm, gates, tq, tk):
    b, s, _ = q_cat.shape
    g_blk0 = A_WIDTH // LANES
    assert tq == tk
    last = s // tq - 1
    return pl.pallas_call(
        functools.partial(_mla_attn_kernel, tq=tq, tk=tk),
        grid=(b, B_HEADS, s // tq),
        in_specs=[
            pl.BlockSpec((1, tq, B_QK_PAD), lambda b, h, t: (b, t, h)),
            pl.BlockSpec((1, tq, B_QK_PAD), lambda b, h, t: (b, jnp.minimum(t + 1, last), h)),
            pl.BlockSpec((1, s, B_QK_PAD), lambda b, h, t: (b, 0, h)),
            pl.BlockSpec((1, s, B_V_DIM), lambda b, h, t: (b, 0, h)),
            pl.BlockSpec((META_PAD, B_QK_PAD), lambda b, h, t: (0, h)),
            pl.BlockSpec((META_PAD, B_V_DIM), lambda b, h, t: (0, h)),
            pl.BlockSpec((1, tq, LANES), lambda b, h, t: (b, t, g_blk0 + h)),
        ],
        out_specs=pl.BlockSpec((1, tq, B_V_DIM), lambda b, h, t: (b, t, h)),
        out_shape=jax.ShapeDtypeStruct((b, s, B_WIDTH), jnp.bfloat16),
        scratch_shapes=[pltpu.VMEM((tq, tk), jnp.float32),
                        pltpu.VMEM((tq, tk), jnp.float32),
                        pltpu.VMEM((tq, tk), jnp.float32),
                        pltpu.VMEM((tq, LANES), jnp.float32),
                        pltpu.VMEM((tq, B_V_DIM + LANES), jnp.float32)],
        compiler_params=_cparams(3),
        name="mla_attn",
    )(q_cat, q_cat, k_cat, v, km, vm, gates)


def _out_proj_kernel(oa_ref, ob_ref, x_ref, wa_ref, wb_ref, fw_ref, y_ref):
    d = jnp.dot(oa_ref[...], wa_ref[...], preferred_element_type=jnp.float32)
    d = d + jnp.dot(ob_ref[...], wb_ref[...], preferred_element_type=jnp.float32)
    hres = x_ref[...] + d
    r = lax.rsqrt(jnp.mean(hres * hres, axis=-1, keepdims=True) + RMS_EPS)
    y_ref[...] = (hres * r) * fw_ref[...]


def _out_proj(oa, ob, x, wa, wb, fw, tm):
    rows, d = x.shape
    full = lambda a: pl.BlockSpec(a.shape, lambda i: (0, 0))
    return pl.pallas_call(
        _out_proj_kernel,
        grid=(rows // tm,),
        in_specs=[pl.BlockSpec((tm, oa.shape[1]), lambda i: (i, 0)),
                  pl.BlockSpec((tm, ob.shape[1]), lambda i: (i, 0)),
                  pl.BlockSpec((tm, d), lambda i: (i, 0)),
                  full(wa), full(wb), full(fw)],
        out_specs=pl.BlockSpec((tm, d), lambda i: (i, 0)),
        out_shape=jax.ShapeDtypeStruct((rows, d), jnp.float32),
        compiler_params=_cparams(1),
        name="out_proj",
    )(oa, ob, x, wa, wb, fw)


def _rot_half_cols(w):
    half = w.shape[-1] // 2
    return jnp.concatenate([-w[..., half:], w[..., :half]], axis=-1)


def _pad_cols(w, width):
    return jnp.pad(w, ((0, 0), (0, width - w.shape[1])))


def kernel(x, meta_tokens, attn_norm_w, w_in, diff_lambda, diff_subln_w, mla_q_norm_w, w_uq,
           mla_kv_norm_w, w_ukv, w_out, final_norm_w):
    bsz, seq, d = x.shape
    bf16 = jnp.bfloat16
    l = 0
    lam_init = 0.8 - 0.6 * math.exp(-0.3 * l)

    wi = w_in[l]
    a_q, a_k, a_v, a_g, b_cq, b_ckv, b_kr, b_g = jnp.split(
        wi, [1024, 2048, 3072, 4096, 4608, 4864, 4928], axis=1)
    w_qkv = jnp.concatenate([a_q * (A_QK_DIM ** -0.5 * LOG2E), a_k, a_v], axis=1).astype(bf16)
    w_gates = jnp.concatenate([a_g, b_g], axis=1).astype(bf16)
    w_small = jnp.concatenate([b_cq, b_ckv, _pad_cols(b_kr, LANES),
                               _pad_cols(_rot_half_cols(b_kr), LANES)], axis=1).astype(bf16)

    wuq = w_uq[l].reshape(B_Q_LORA, B_HEADS, B_NOPE + B_ROPE)
    wq_main = jnp.pad(wuq, ((0, 0), (0, 0), (0, B_QK_PAD - B_NOPE - B_ROPE)))
    wq_main = wq_main.reshape(B_Q_LORA, B_HEADS * B_QK_PAD).astype(bf16)
    wq_rot = jnp.pad(_rot_half_cols(wuq[..., B_NOPE:]), ((0, 0), (0, 0), (0, LANES - B_ROPE)))
    wq_rot = wq_rot.reshape(B_Q_LORA, B_HEADS * LANES).astype(bf16)
    wukv = w_ukv[l].reshape(B_KV_LORA, B_HEADS, B_NOPE + B_V_DIM)
    wk = wukv[..., :B_NOPE].reshape(B_KV_LORA, B_HEADS * B_NOPE).astype(bf16)
    wv = wukv[..., B_NOPE:].reshape(B_KV_LORA, B_WIDTH).astype(bf16)
    wo = w_out[l].astype(bf16)

    half = B_ROPE // 2
    inv = ROPE_THETA ** (-jnp.arange(half, dtype=jnp.float32) / half)
    inv = jnp.tile(inv, LANES // half)[None, :]
    slopes = jnp.exp2(-8.0 * jnp.arange(1, A_HEADS + 1, dtype=jnp.float32) / A_HEADS)

    nw = attn_norm_w[l][None, :]
    qnw = mla_q_norm_w[l][None, :]
    kvnw = mla_kv_norm_w[l][None, :]

    x2 = x.reshape(bsz * seq, d)
    qkv = _norm_matmul(x2, nw, w_qkv, bf16, 512, 1024)
    gates = _norm_matmul(x2, nw, w_gates, jnp.float32, 512, 1024)
    small = _norm_matmul(x2, nw, w_small, jnp.float32, 512, 1024)
    meta = meta_tokens.astype(x.dtype)
    qkv_m = _norm_matmul(meta, nw, w_qkv, bf16, N_META, 1024)
    small_m = _norm_matmul(meta, nw, w_small, jnp.float32, N_META, 1024)

    q_cat, k_cat, v_b = _mla_proj(small, qnw, kvnw, inv, wq_main, wq_rot, wk, wv, 512, seq, N_META)
    _, k_cat_m, v_b_m = _mla_proj(small_m, qnw, kvnw, inv, wq_main, wq_rot, wk, wv, N_META,
                                  N_META, 0)

    pad_meta = lambda a: jnp.pad(a, ((0, META_PAD - N_META), (0, 0)))

    o_a = _diff_attn(slopes, qkv.reshape(bsz, seq, -1), pad_meta(qkv_m),
                     gates.reshape(bsz, seq, -1), diff_lambda[l].astype(jnp.float32),
                     diff_subln_w[l][None, :], 512, 512, lam_init)
    o_b = _mla_attn(q_cat.reshape(bsz, seq, -1), k_cat.reshape(bsz, seq, -1),
                    v_b.reshape(bsz, seq, -1), pad_meta(k_cat_m), pad_meta(v_b_m),
                    gates.reshape(bsz, seq, -1), 1024, 1024)

    y = _out_proj(o_a.reshape(bsz * seq, -1), o_b.reshape(bsz * seq, -1), x2,
                  wo[:A_WIDTH], wo[A_WIDTH:], final_norm_w[None, :], 256)
    return y.reshape(bsz, seq, d)
```

```python
import functools
import math

import jax
import jax.numpy as jnp
from jax import lax
from jax.experimental import pallas as pl
from jax.experimental.pallas import tpu as pltpu

N_META = 16
RMS_EPS = 1e-6
ROPE_THETA = 10000.0
LOG2E = math.log2(math.e)

A_HEADS = 8
A_QK_DIM = 64
A_V_DIM = 128
A_WIDTH = A_HEADS * A_V_DIM
B_HEADS = 8
B_Q_LORA = 512
B_KV_LORA = 256
B_NOPE = 128
B_ROPE = 64
B_V_DIM = 128
B_WIDTH = B_HEADS * B_V_DIM
B_QK_PAD = 256

LANES = 128
META_PAD = 128
POS_RADIX = 256
VMEM_LIMIT = 48 * 1024 * 1024

_NT = (((1,), (1,)), ((), ()))


def _cparams(n_grid):
    return pltpu.CompilerParams(
        dimension_semantics=("arbitrary",) * n_grid, vmem_limit_bytes=VMEM_LIMIT)


def _in_proj_kernel(x_ref, nw_ref, w_ref, qkv_ref, gates_ref, small_ref, xn_ref,
                    *, n_qkv, n_gates):
    j = pl.program_id(1)

    @pl.when(j == 0)
    def _():
        x = x_ref[...]
        r = lax.rsqrt(jnp.mean(x * x, axis=-1, keepdims=True) + RMS_EPS)
        xn_ref[...] = ((x * r) * nw_ref[...]).astype(xn_ref.dtype)

    def tile():
        return jnp.dot(xn_ref[...], w_ref[...], preferred_element_type=jnp.float32)

    @pl.when(j < n_qkv)
    def _():
        qkv_ref[...] = tile().astype(qkv_ref.dtype)

    @pl.when(jnp.logical_and(j >= n_qkv, j < n_qkv + n_gates))
    def _():
        gates_ref[...] = tile()

    @pl.when(j >= n_qkv + n_gates)
    def _():
        small_ref[...] = tile()


def _in_proj(x, nw, w_all, widths, tm, tn):
    rows, k = x.shape
    n_qkv, n_gates, n_small = (w // tn for w in widths)
    assert (n_qkv * tn, n_gates * tn, n_small * tn) == tuple(widths)
    return pl.pallas_call(
        functools.partial(_in_proj_kernel, n_qkv=n_qkv, n_gates=n_gates),
        grid=(rows // tm, n_qkv + n_gates + n_small),
        in_specs=[
            pl.BlockSpec((tm, k), lambda i, j: (i, 0)),
            pl.BlockSpec((1, k), lambda i, j: (0, 0)),
            pl.BlockSpec((k, tn), lambda i, j: (0, j)),
        ],
        out_specs=[
            pl.BlockSpec((tm, tn), lambda i, j: (i, jnp.minimum(j, n_qkv - 1))),
            pl.BlockSpec((tm, tn), lambda i, j: (i, jnp.clip(j - n_qkv, 0, n_gates - 1))),
            pl.BlockSpec((tm, tn),
                         lambda i, j: (i, jnp.clip(j - n_qkv - n_gates, 0, n_small - 1))),
        ],
        out_shape=[jax.ShapeDtypeStruct((rows, widths[0]), jnp.bfloat16),
                   jax.ShapeDtypeStruct((rows, widths[1]), jnp.float32),
                   jax.ShapeDtypeStruct((rows, widths[2]), jnp.float32)],
        scratch_shapes=[pltpu.VMEM((tm, k), jnp.bfloat16)],
        compiler_params=_cparams(2),
        name="in_proj",
    )(x, nw, w_all)


def _mla_proj_kernel(small_ref, qnw_ref, kvnw_ref, inv_ref, wq_ref, wqr_ref, wk_ref, wv_ref,
                     q_ref, k_ref, v_ref, *, tm, seq, pos0, scale):
    def rms(x, w):
        r = lax.rsqrt(jnp.mean(x * x, axis=-1, keepdims=True) + RMS_EPS)
        return (x * r) * w

    cq = rms(small_ref[:, 0:B_Q_LORA], qnw_ref[...]).astype(jnp.bfloat16)
    ckv = rms(small_ref[:, B_Q_LORA:B_Q_LORA + B_KV_LORA], kvnw_ref[...]).astype(jnp.bfloat16)
    kr = small_ref[:, 768:896]
    kr_rot = small_ref[:, 896:1024]

    row = lax.broadcasted_iota(jnp.int32, (tm, LANES), 0)
    pos = (row + ((pl.program_id(0) * tm) % seq + pos0)).astype(jnp.float32)
    ang = pos * inv_ref[...]
    cos, sin = jnp.cos(ang), jnp.sin(ang)

    q1 = jnp.dot(cq, wq_ref[...], preferred_element_type=jnp.float32)
    q2 = jnp.dot(cq, wqr_ref[...], preferred_element_type=jnp.float32)
    kn = jnp.dot(ckv, wk_ref[...], preferred_element_type=jnp.float32)
    v_ref[...] = jnp.dot(ckv, wv_ref[...], preferred_element_type=jnp.float32).astype(v_ref.dtype)
    kpe = (kr * cos + kr_rot * sin).astype(k_ref.dtype)
    for h in range(B_HEADS):
        c0 = h * B_QK_PAD
        q_ref[:, c0:c0 + LANES] = (q1[:, c0:c0 + LANES] * scale).astype(q_ref.dtype)
        qpe = q1[:, c0 + LANES:c0 + 2 * LANES] * cos + q2[:, h * LANES:(h + 1) * LANES] * sin
        q_ref[:, c0 + LANES:c0 + 2 * LANES] = (qpe * scale).astype(q_ref.dtype)
        k_ref[:, c0:c0 + LANES] = kn[:, h * LANES:(h + 1) * LANES].astype(k_ref.dtype)
        k_ref[:, c0 + LANES:c0 + 2 * LANES] = kpe


def _mla_proj(small, qnw, kvnw, inv, wq, wqr, wk, wv, tm, seq, pos0):
    rows = small.shape[0]
    full = lambda a: pl.BlockSpec(a.shape, lambda i: (0, 0))
    scale = (B_NOPE + B_ROPE) ** -0.5 * LOG2E
    return pl.pallas_call(
        functools.partial(_mla_proj_kernel, tm=tm, seq=seq, pos0=pos0, scale=scale),
        grid=(rows // tm,),
        in_specs=[pl.BlockSpec((tm, small.shape[1]), lambda i: (i, 0)),
                  full(qnw), full(kvnw), full(inv), full(wq), full(wqr), full(wk), full(wv)],
        out_specs=[pl.BlockSpec((tm, B_HEADS * B_QK_PAD), lambda i: (i, 0)),
                   pl.BlockSpec((tm, B_HEADS * B_QK_PAD), lambda i: (i, 0)),
                   pl.BlockSpec((tm, B_WIDTH), lambda i: (i, 0))],
        out_shape=[jax.ShapeDtypeStruct((rows, B_HEADS * B_QK_PAD), jnp.bfloat16),
                   jax.ShapeDtypeStruct((rows, B_HEADS * B_QK_PAD), jnp.bfloat16),
                   jax.ShapeDtypeStruct((rows, B_WIDTH), jnp.bfloat16)],
        compiler_params=_cparams(1),
        name="mla_proj",
    )(small, qnw, kvnw, inv, wq, wqr, wk, wv)


def _silu(g):
    return g * (1.0 / (1.0 + jnp.exp(-g)))


def _lanes(x, width):
    return jnp.tile(x, (1, width // LANES))


def _with_ones(v):
    return jnp.concatenate([v, jnp.ones((v.shape[0], LANES), v.dtype)], axis=1)


def _online_step(s, v, m_ref, acc_ref):
    tk = s.shape[-1]
    m_prev = m_ref[...]
    m_new = jnp.maximum(m_prev, jnp.max(s, axis=-1, keepdims=True))
    alpha = jnp.exp2(m_prev - m_new)
    p = jnp.exp2(s - _lanes(m_new, tk))
    pv = jnp.dot(p.astype(v.dtype), _with_ones(v), preferred_element_type=jnp.float32)
    acc_ref[...] = _lanes(alpha, acc_ref.shape[-1]) * acc_ref[...] + pv
    m_ref[...] = m_new


def _first_step(s, v, m_ref, acc_ref):
    tk = s.shape[-1]
    m_new = jnp.broadcast_to(jnp.max(s, axis=-1, keepdims=True), m_ref.shape)
    p = jnp.exp2(s - _lanes(m_new, tk))
    acc_ref[...] = jnp.dot(p.astype(v.dtype), _with_ones(v), preferred_element_type=jnp.float32)
    m_ref[...] = m_new


def _causal_flash(t, tk, scores, next_scores0, meta_scores, diag_mask,
                  v_ref, vm_ref, sa_ref, sb_ref, sc_ref, m_ref, acc_ref):
    def softmax_pv(s_ref, kt, masked):
        k0 = pl.multiple_of(kt * tk, tk)
        s = s_ref[...]
        if masked:
            s = jnp.where(diag_mask(), s, -jnp.inf)
        _online_step(s, v_ref[0, pl.ds(k0, tk), :], m_ref, acc_ref)

    @pl.when(t == 0)
    def _():
        sc_ref[...] = scores(0)
        _first_step(meta_scores(), vm_ref[...], m_ref, acc_ref)
        softmax_pv(sc_ref, 0, True)
        sc_ref[...] = next_scores0()

    @pl.when(t > 0)
    def _():
        sb_ref[...] = scores(1)
        _first_step(sc_ref[...], v_ref[0, 0:tk, :], m_ref, acc_ref)
        _online_step(meta_scores(), vm_ref[...], m_ref, acc_ref)

        def pair(j, carry):
            sa_ref[...] = scores(2 * j + 2)
            softmax_pv(sb_ref, 2 * j + 1, False)
            sb_ref[...] = scores(2 * j + 3)
            softmax_pv(sa_ref, 2 * j + 2, False)
            return carry

        n_pairs = (t - 1) // 2
        lax.fori_loop(0, n_pairs, pair, 0)
        kt = 2 * n_pairs + 1

        @pl.when(kt == t)
        def _():
            sc_ref[...] = next_scores0()
            softmax_pv(sb_ref, kt, True)

        @pl.when(kt < t)
        def _():
            sa_ref[...] = scores(kt + 1)
            softmax_pv(sb_ref, kt, False)
            sc_ref[...] = next_scores0()
            softmax_pv(sa_ref, kt + 1, True)


def _diff_attn_kernel(slopes_ref, q_ref, qn_ref, k_ref, v_ref, pk_ref, km_ref, vm_ref, pkm_ref,
                      g_ref, lam_ref, sw_ref, o_ref, qs_ref, sa_ref, sb_ref, sc_ref, m_ref,
                      acc_ref, *, tq, tk, lam_init):
    h = pl.program_id(1)
    t = pl.program_id(2)
    rows = 2 * tq

    slope = jnp.full((1, LANES), slopes_ref[h] * LOG2E, jnp.float32)
    s0 = slope.astype(jnp.bfloat16).astype(jnp.float32)
    s1 = (slope - s0).astype(jnp.bfloat16).astype(jnp.float32)
    s2 = (slope - s0 - s1).astype(jnp.bfloat16).astype(jnp.float32)
    ln = lax.broadcasted_iota(jnp.int32, (1, LANES), 1)
    piece = jnp.where(ln % 3 == 0, s0, jnp.where(ln % 3 == 1, s1, s2))
    q_pos = jnp.where(ln < 3, piece * float(POS_RADIX), jnp.where(ln < 6, piece, 0.0))
    q_pos = jnp.broadcast_to(q_pos, (rows, LANES)).astype(jnp.bfloat16)

    def stacked(qb):
        lane = lax.broadcasted_iota(jnp.int32, (tq, LANES), 1)
        zero = jnp.zeros_like(qb)
        qs = jnp.concatenate([jnp.where(lane < A_QK_DIM, qb, zero),
                              jnp.where(lane >= A_QK_DIM, qb, zero)], axis=0)
        return jnp.concatenate([qs, q_pos], axis=1)

    def tile_scores(qs, k0):
        kk = jnp.concatenate([k_ref[0, pl.ds(k0, tk), :], pk_ref[pl.ds(k0, tk), :]], axis=1)
        return lax.dot_general(qs, kk, _NT, preferred_element_type=jnp.float32)

    qs_ref[...] = stacked(q_ref[0])

    def scores(kt):
        return tile_scores(qs_ref[...], pl.multiple_of(kt * tk, tk))

    def next_scores0():
        return tile_scores(stacked(qn_ref[0]), 0)

    def meta_scores():
        kk = jnp.concatenate([km_ref[...], pkm_ref[...]], axis=1)
        s = lax.dot_general(qs_ref[...], kk, _NT, preferred_element_type=jnp.float32)
        cm = lax.broadcasted_iota(jnp.int32, (rows, META_PAD), 1)
        return jnp.where(cm < N_META, s, -jnp.inf)

    def diag_mask():
        r = lax.broadcasted_iota(jnp.int32, (rows, tk), 0)
        r = jnp.where(r >= tq, r - tq, r)
        c = lax.broadcasted_iota(jnp.int32, (rows, tk), 1)
        return c <= r

    _causal_flash(t, tk, scores, next_scores0, meta_scores, diag_mask,
                  v_ref, vm_ref, sa_ref, sb_ref, sc_ref, m_ref, acc_ref)

    lp = lam_ref[...]
    lam = (jnp.exp(jnp.sum(lp[0:1] * lp[1:2], axis=-1, keepdims=True))
           - jnp.exp(jnp.sum(lp[2:3] * lp[3:4], axis=-1, keepdims=True)) + lam_init)
    o = acc_ref[:, 0:A_V_DIM] / acc_ref[:, A_V_DIM:A_V_DIM + LANES]
    o = o[0:tq] - lam * o[tq:rows]
    o = (o * lax.rsqrt(jnp.mean(o * o, axis=-1, keepdims=True) + RMS_EPS)) * sw_ref[...]
    o = o * (1.0 - lam_init)
    o_ref[0] = (o * _silu(g_ref[0])).astype(o_ref.dtype)


def _pos_columns(pos):
    a = (pos // POS_RADIX).astype(jnp.float32)[:, None]
    b = (pos % POS_RADIX).astype(jnp.float32)[:, None]
    ln = jnp.arange(LANES)[None, :]
    return jnp.where(ln < 3, a, jnp.where(ln < 6, b, 0.0)).astype(jnp.bfloat16)


def _diff_attn(slopes, qkv, kv_meta, gates, lam_p, subln_w, tq, tk, lam_init):
    b, s, _ = qkv.shape
    hq = A_HEADS
    pos_x = _pos_columns(jnp.arange(s, dtype=jnp.int32) + N_META)
    pos_m = _pos_columns(jnp.arange(META_PAD, dtype=jnp.int32))
    assert tq == tk
    last = s // tq - 1
    return pl.pallas_call(
        functools.partial(_diff_attn_kernel, tq=tq, tk=tk, lam_init=lam_init),
        grid=(b, hq, s // tq),
        in_specs=[
            pl.BlockSpec(memory_space=pltpu.SMEM),
            pl.BlockSpec((1, tq, LANES), lambda b, h, t: (b, t, h)),
            pl.BlockSpec((1, tq, LANES), lambda b, h, t: (b, jnp.minimum(t + 1, last), h)),
            pl.BlockSpec((1, s, LANES), lambda b, h, t: (b, 0, hq + h)),
            pl.BlockSpec((1, s, LANES), lambda b, h, t: (b, 0, 2 * hq + h)),
            pl.BlockSpec((s, LANES), lambda b, h, t: (0, 0)),
            pl.BlockSpec((META_PAD, LANES), lambda b, h, t: (0, hq + h)),
            pl.BlockSpec((META_PAD, LANES), lambda b, h, t: (0, 2 * hq + h)),
            pl.BlockSpec((META_PAD, LANES), lambda b, h, t: (0, 0)),
            pl.BlockSpec((1, tq, LANES), lambda b, h, t: (b, t, h)),
            pl.BlockSpec(lam_p.shape, lambda b, h, t: (0, 0)),
            pl.BlockSpec(subln_w.shape, lambda b, h, t: (0, 0)),
        ],
        out_specs=pl.BlockSpec((1, tq, LANES), lambda b, h, t: (b, t, h)),
        out_shape=jax.ShapeDtypeStruct((b, s, A_WIDTH), jnp.bfloat16),
        scratch_shapes=[pltpu.VMEM((2 * tq, 2 * LANES), jnp.bfloat16),
                        pltpu.VMEM((2 * tq, tk), jnp.float32),
                        pltpu.VMEM((2 * tq, tk), jnp.float32),
                        pltpu.VMEM((2 * tq, tk), jnp.float32),
                        pltpu.VMEM((2 * tq, LANES), jnp.float32),
                        pltpu.VMEM((2 * tq, A_V_DIM + LANES), jnp.float32)],
        compiler_params=_cparams(3),
        name="diff_attn",
    )(slopes, qkv, qkv, qkv, qkv, pos_x, kv_meta, kv_meta, pos_m, gates, lam_p, subln_w)


def _mla_attn_kernel(q_ref, qn_ref, k_ref, v_ref, km_ref, vm_ref, g_ref, o_ref,
                     sa_ref, sb_ref, sc_ref, m_ref, acc_ref, *, tq, tk):
    t = pl.program_id(2)

    def scores(kt):
        k0 = pl.multiple_of(kt * tk, tk)
        return lax.dot_general(q_ref[0], k_ref[0, pl.ds(k0, tk), :], _NT,
                               preferred_element_type=jnp.float32)

    def next_scores0():
        return lax.dot_general(qn_ref[0], k_ref[0, 0:tk, :], _NT,
                               preferred_element_type=jnp.float32)

    def meta_scores():
        s = lax.dot_general(q_ref[0], km_ref[...], _NT, preferred_element_type=jnp.float32)
        cm = lax.broadcasted_iota(jnp.int32, (tq, META_PAD), 1)
        return jnp.where(cm < N_META, s, -jnp.inf)

    def diag_mask():
        r = lax.broadcasted_iota(jnp.int32, (tq, tk), 0)
        c = lax.broadcasted_iota(jnp.int32, (tq, tk), 1)
        return c <= r

    _causal_flash(t, tk, scores, next_scores0, meta_scores, diag_mask,
                  v_ref, vm_ref, sa_ref, sb_ref, sc_ref, m_ref, acc_ref)

    o = acc_ref[:, 0:B_V_DIM] / acc_ref[:, B_V_DIM:B_V_DIM + LANES]
    o_ref[0] = (o * _silu(g_ref[0])).astype(o_ref.dtype)


def _mla_attn(q_cat, k_cat, v, km, vm, gates, tq, tk):
    b, s, _ = q_cat.shape
    g_blk0 = A_WIDTH // LANES
    assert tq == tk
    last = s // tq - 1
    return pl.pallas_call(
        functools.partial(_mla_attn_kernel, tq=tq, tk=tk),
        grid=(b, B_HEADS, s // tq),
        in_specs=[
            pl.BlockSpec((1, tq, B_QK_PAD), lambda b, h, t: (b, t, h)),
            pl.BlockSpec((1, tq, B_QK_PAD), lambda b, h, t: (b, jnp.minimum(t + 1, last), h)),
            pl.BlockSpec((1, s, B_QK_PAD), lambda b, h, t: (b, 0, h)),
            pl.BlockSpec((1, s, B_V_DIM), lambda b, h, t: (b, 0, h)),
            pl.BlockSpec((META_PAD, B_QK_PAD), lambda b, h, t: (0, h)),
            pl.BlockSpec((META_PAD, B_V_DIM), lambda b, h, t: (0, h)),
            pl.BlockSpec((1, tq, LANES), lambda b, h, t: (b, t, g_blk0 + h)),
        ],
        out_specs=pl.BlockSpec((1, tq, B_V_DIM), lambda b, h, t: (b, t, h)),
        out_shape=jax.ShapeDtypeStruct((b, s, B_WIDTH), jnp.bfloat16),
        scratch_shapes=[pltpu.VMEM((tq, tk), jnp.float32),
                        pltpu.VMEM((tq, tk), jnp.float32),
                        pltpu.VMEM((tq, tk), jnp.float32),
                        pltpu.VMEM((tq, LANES), jnp.float32),
                        pltpu.VMEM((tq, B_V_DIM + LANES), jnp.float32)],
        compiler_params=_cparams(3),
        name="mla_attn",
    )(q_cat, q_cat, k_cat, v, km, vm, gates)


def _out_proj_kernel(oa_ref, ob_ref, x_ref, wa_ref, wb_ref, fw_ref, y_ref):
    d = jnp.dot(oa_ref[...], wa_ref[...], preferred_element_type=jnp.float32)
    d = d + jnp.dot(ob_ref[...], wb_ref[...], preferred_element_type=jnp.float32)
    hres = x_ref[...] + d
    r = lax.rsqrt(jnp.mean(hres * hres, axis=-1, keepdims=True) + RMS_EPS)
    y_ref[...] = (hres * r) * fw_ref[...]


def _out_proj(oa, ob, x, wa, wb, fw, tm):
    rows, d = x.shape
    full = lambda a: pl.BlockSpec(a.shape, lambda i: (0, 0))
    return pl.pallas_call(
        _out_proj_kernel,
        grid=(rows // tm,),
        in_specs=[pl.BlockSpec((tm, oa.shape[1]), lambda i: (i, 0)),
                  pl.BlockSpec((tm, ob.shape[1]), lambda i: (i, 0)),
                  pl.BlockSpec((tm, d), lambda i: (i, 0)),
                  full(wa), full(wb), full(fw)],
        out_specs=pl.BlockSpec((tm, d), lambda i: (i, 0)),
        out_shape=jax.ShapeDtypeStruct((rows, d), jnp.float32),
        compiler_params=_cparams(1),
        name="out_proj",
    )(oa, ob, x, wa, wb, fw)


def _rot_half_cols(w):
    half = w.shape[-1] // 2
    return jnp.concatenate([-w[..., half:], w[..., :half]], axis=-1)


def _pad_cols(w, width):
    return jnp.pad(w, ((0, 0), (0, width - w.shape[1])))


def kernel(x, meta_tokens, attn_norm_w, w_in, diff_lambda, diff_subln_w, mla_q_norm_w, w_uq,
           mla_kv_norm_w, w_ukv, w_out, final_norm_w):
    bsz, seq, d = x.shape
    bf16 = jnp.bfloat16
    l = 0
    lam_init = 0.8 - 0.6 * math.exp(-0.3 * l)

    wi = w_in[l]
    a_q, a_k, a_v, a_g, b_cq, b_ckv, b_kr, b_g = jnp.split(
        wi, [1024, 2048, 3072, 4096, 4608, 4864, 4928], axis=1)
    w_all = jnp.concatenate(
        [a_q * (A_QK_DIM ** -0.5 * LOG2E), a_k, a_v,
         a_g, b_g,
         b_cq, b_ckv, _pad_cols(b_kr, LANES), _pad_cols(_rot_half_cols(b_kr), LANES)],
        axis=1).astype(bf16)
    widths = (3 * A_WIDTH, A_WIDTH + B_WIDTH, B_Q_LORA + B_KV_LORA + 2 * LANES)

    wuq = w_uq[l].reshape(B_Q_LORA, B_HEADS, B_NOPE + B_ROPE)
    wq_main = jnp.pad(wuq, ((0, 0), (0, 0), (0, B_QK_PAD - B_NOPE - B_ROPE)))
    wq_main = wq_main.reshape(B_Q_LORA, B_HEADS * B_QK_PAD).astype(bf16)
    wq_rot = jnp.pad(_rot_half_cols(wuq[..., B_NOPE:]), ((0, 0), (0, 0), (0, LANES - B_ROPE)))
    wq_rot = wq_rot.reshape(B_Q_LORA, B_HEADS * LANES).astype(bf16)
    wukv = w_ukv[l].reshape(B_KV_LORA, B_HEADS, B_NOPE + B_V_DIM)
    wk = wukv[..., :B_NOPE].reshape(B_KV_LORA, B_HEADS * B_NOPE).astype(bf16)
    wv = wukv[..., B_NOPE:].reshape(B_KV_LORA, B_WIDTH).astype(bf16)
    wo = w_out[l].astype(bf16)

    half = B_ROPE // 2
    inv = ROPE_THETA ** (-jnp.arange(half, dtype=jnp.float32) / half)
    inv = jnp.tile(inv, LANES // half)[None, :]
    slopes = jnp.exp2(-8.0 * jnp.arange(1, A_HEADS + 1, dtype=jnp.float32) / A_HEADS)

    nw = attn_norm_w[l][None, :]
    qnw = mla_q_norm_w[l][None, :]
    kvnw = mla_kv_norm_w[l][None, :]

    x2 = x.reshape(bsz * seq, d)
    qkv, gates, small = _in_proj(x2, nw, w_all, widths, 1024, 512)
    meta = meta_tokens.astype(x.dtype)
    qkv_m, _, small_m = _in_proj(meta, nw, w_all, widths, N_META, 512)

    q_cat, k_cat, v_b = _mla_proj(small, qnw, kvnw, inv, wq_main, wq_rot, wk, wv, 512, seq, N_META)
    _, k_cat_m, v_b_m = _mla_proj(small_m, qnw, kvnw, inv, wq_main, wq_rot, wk, wv, N_META,
                                  N_META, 0)

    pad_meta = lambda a: jnp.pad(a, ((0, META_PAD - N_META), (0, 0)))

    o_a = _diff_attn(slopes, qkv.reshape(bsz, seq, -1), pad_meta(qkv_m),
                     gates.reshape(bsz, seq, -1), diff_lambda[l].astype(jnp.float32),
                     diff_subln_w[l][None, :], 512, 512, lam_init)
    o_b = _mla_attn(q_cat.reshape(bsz, seq, -1), k_cat.reshape(bsz, seq, -1),
                    v_b.reshape(bsz, seq, -1), pad_meta(k_cat_m), pad_meta(v_b_m),
                    gates.reshape(bsz, seq, -1), 1024, 1024)

    y = _out_proj(o_a.reshape(bsz * seq, -1), o_b.reshape(bsz * seq, -1), x2,
                  wo[:A_WIDTH], wo[A_WIDTH:], final_norm_w[None, :], 512)
    return y.reshape(bsz, seq, d)
```

```python
import functools
import math

import jax
import jax.numpy as jnp
from jax import lax
from jax.experimental import pallas as pl
from jax.experimental.pallas import tpu as pltpu

N_META = 16
RMS_EPS = 1e-6
ROPE_THETA = 10000.0
LOG2E = math.log2(math.e)

A_HEADS = 8
A_QK_DIM = 64
A_V_DIM = 128
A_WIDTH = A_HEADS * A_V_DIM
B_HEADS = 8
B_Q_LORA = 512
B_KV_LORA = 256
B_NOPE = 128
B_ROPE = 64
B_V_DIM = 128
B_WIDTH = B_HEADS * B_V_DIM
B_QK_PAD = 256

LANES = 128
META_PAD = 128
IN_PROJ_TN = 512
POS_RADIX = 256
VMEM_LIMIT = 48 * 1024 * 1024

_NT = (((1,), (1,)), ((), ()))


def _cparams(n_grid):
    return pltpu.CompilerParams(
        dimension_semantics=("arbitrary",) * n_grid, vmem_limit_bytes=VMEM_LIMIT)


def _in_proj_kernel(x_ref, nw_ref, w_ref, qkv_ref, gates_ref, small_ref, xn_ref,
                    *, n_qkv, n_gates):
    j = pl.program_id(1)

    @pl.when(j == 0)
    def _():
        x = x_ref[...]
        r = lax.rsqrt(jnp.mean(x * x, axis=-1, keepdims=True) + RMS_EPS)
        xn_ref[...] = ((x * r) * nw_ref[...]).astype(xn_ref.dtype)

    def tile():
        return jnp.dot(xn_ref[...], w_ref[0], preferred_element_type=jnp.float32)

    @pl.when(j < n_qkv)
    def _():
        qkv_ref[...] = tile().astype(qkv_ref.dtype)

    @pl.when(jnp.logical_and(j >= n_qkv, j < n_qkv + n_gates))
    def _():
        gates_ref[...] = tile()

    @pl.when(j >= n_qkv + n_gates)
    def _():
        small_ref[...] = tile()


def _in_proj(x, nw, w_tiles, widths, tm):
    rows, k = x.shape
    tn = w_tiles.shape[2]
    n_qkv, n_gates, n_small = (w // tn for w in widths)
    assert (n_qkv * tn, n_gates * tn, n_small * tn) == tuple(widths)
    return pl.pallas_call(
        functools.partial(_in_proj_kernel, n_qkv=n_qkv, n_gates=n_gates),
        grid=(rows // tm, n_qkv + n_gates + n_small),
        in_specs=[
            pl.BlockSpec((tm, k), lambda i, j: (i, 0)),
            pl.BlockSpec((1, k), lambda i, j: (0, 0)),
            pl.BlockSpec((1, k, tn), lambda i, j: (j, 0, 0)),
        ],
        out_specs=[
            pl.BlockSpec((tm, tn), lambda i, j: (i, jnp.minimum(j, n_qkv - 1))),
            pl.BlockSpec((tm, tn), lambda i, j: (i, jnp.clip(j - n_qkv, 0, n_gates - 1))),
            pl.BlockSpec((tm, tn),
                         lambda i, j: (i, jnp.clip(j - n_qkv - n_gates, 0, n_small - 1))),
        ],
        out_shape=[jax.ShapeDtypeStruct((rows, widths[0]), jnp.bfloat16),
                   jax.ShapeDtypeStruct((rows, widths[1]), jnp.float32),
                   jax.ShapeDtypeStruct((rows, widths[2]), jnp.float32)],
        scratch_shapes=[pltpu.VMEM((tm, k), jnp.bfloat16)],
        compiler_params=_cparams(2),
        name="in_proj",
    )(x, nw, w_tiles)


def _mla_proj_kernel(small_ref, qnw_ref, kvnw_ref, inv_ref, wq_ref, wqr_ref, wk_ref, wv_ref,
                     q_ref, k_ref, v_ref, *, tm, seq, pos0, scale):
    def rms(x, w):
        r = lax.rsqrt(jnp.mean(x * x, axis=-1, keepdims=True) + RMS_EPS)
        return (x * r) * w

    cq = rms(small_ref[:, 0:B_Q_LORA], qnw_ref[...]).astype(jnp.bfloat16)
    ckv = rms(small_ref[:, B_Q_LORA:B_Q_LORA + B_KV_LORA], kvnw_ref[...]).astype(jnp.bfloat16)
    kr = small_ref[:, 768:896]
    kr_rot = small_ref[:, 896:1024]

    row = lax.broadcasted_iota(jnp.int32, (tm, LANES), 0)
    pos = (row + ((pl.program_id(0) * tm) % seq + pos0)).astype(jnp.float32)
    ang = pos * inv_ref[...]
    cos, sin = jnp.cos(ang), jnp.sin(ang)

    q1 = jnp.dot(cq, wq_ref[...], preferred_element_type=jnp.float32)
    q2 = jnp.dot(cq, wqr_ref[...], preferred_element_type=jnp.float32)
    kn = jnp.dot(ckv, wk_ref[...], preferred_element_type=jnp.float32)
    v_ref[...] = jnp.dot(ckv, wv_ref[...], preferred_element_type=jnp.float32).astype(v_ref.dtype)
    kpe = (kr * cos + kr_rot * sin).astype(k_ref.dtype)
    for h in range(B_HEADS):
        c0 = h * B_QK_PAD
        q_ref[:, c0:c0 + LANES] = (q1[:, c0:c0 + LANES] * scale).astype(q_ref.dtype)
        qpe = q1[:, c0 + LANES:c0 + 2 * LANES] * cos + q2[:, h * LANES:(h + 1) * LANES] * sin
        q_ref[:, c0 + LANES:c0 + 2 * LANES] = (qpe * scale).astype(q_ref.dtype)
        k_ref[:, c0:c0 + LANES] = kn[:, h * LANES:(h + 1) * LANES].astype(k_ref.dtype)
        k_ref[:, c0 + LANES:c0 + 2 * LANES] = kpe


def _mla_proj(small, qnw, kvnw, inv, wq, wqr, wk, wv, tm, seq, pos0):
    rows = small.shape[0]
    full = lambda a: pl.BlockSpec(a.shape, lambda i: (0, 0))
    scale = (B_NOPE + B_ROPE) ** -0.5 * LOG2E
    return pl.pallas_call(
        functools.partial(_mla_proj_kernel, tm=tm, seq=seq, pos0=pos0, scale=scale),
        grid=(rows // tm,),
        in_specs=[pl.BlockSpec((tm, small.shape[1]), lambda i: (i, 0)),
                  full(qnw), full(kvnw), full(inv), full(wq), full(wqr), full(wk), full(wv)],
        out_specs=[pl.BlockSpec((tm, B_HEADS * B_QK_PAD), lambda i: (i, 0)),
                   pl.BlockSpec((tm, B_HEADS * B_QK_PAD), lambda i: (i, 0)),
                   pl.BlockSpec((tm, B_WIDTH), lambda i: (i, 0))],
        out_shape=[jax.ShapeDtypeStruct((rows, B_HEADS * B_QK_PAD), jnp.bfloat16),
                   jax.ShapeDtypeStruct((rows, B_HEADS * B_QK_PAD), jnp.bfloat16),
                   jax.ShapeDtypeStruct((rows, B_WIDTH), jnp.bfloat16)],
        compiler_params=_cparams(1),
        name="mla_proj",
    )(small, qnw, kvnw, inv, wq, wqr, wk, wv)


def _silu(g):
    return g * (1.0 / (1.0 + jnp.exp(-g)))


def _lanes(x, width):
    return jnp.tile(x, (1, width // LANES))


def _with_ones(v):
    return jnp.concatenate([v, jnp.ones((v.shape[0], LANES), v.dtype)], axis=1)


def _online_step(s, v, m_ref, acc_ref):
    tk = s.shape[-1]
    m_prev = m_ref[...]
    m_new = jnp.maximum(m_prev, jnp.max(s, axis=-1, keepdims=True))
    alpha = jnp.exp2(m_prev - m_new)
    p = jnp.exp2(s - _lanes(m_new, tk))
    pv = jnp.dot(p.astype(v.dtype), _with_ones(v), preferred_element_type=jnp.float32)
    acc_ref[...] = _lanes(alpha, acc_ref.shape[-1]) * acc_ref[...] + pv
    m_ref[...] = m_new


def _first_step(s, v, m_ref, acc_ref):
    tk = s.shape[-1]
    m_new = jnp.broadcast_to(jnp.max(s, axis=-1, keepdims=True), m_ref.shape)
    p = jnp.exp2(s - _lanes(m_new, tk))
    acc_ref[...] = jnp.dot(p.astype(v.dtype), _with_ones(v), preferred_element_type=jnp.float32)
    m_ref[...] = m_new


def _causal_flash(t, tk, scores, next_scores0, meta_scores, diag_mask,
                  v_ref, vm_ref, sa_ref, sb_ref, sc_ref, m_ref, acc_ref):
    def softmax_pv(s_ref, kt, masked):
        k0 = pl.multiple_of(kt * tk, tk)
        s = s_ref[...]
        if masked:
            s = jnp.where(diag_mask(), s, -jnp.inf)
        _online_step(s, v_ref[0, pl.ds(k0, tk), :], m_ref, acc_ref)

    @pl.when(t == 0)
    def _():
        sc_ref[...] = scores(0)
        _first_step(meta_scores(), vm_ref[...], m_ref, acc_ref)
        softmax_pv(sc_ref, 0, True)
        sc_ref[...] = next_scores0()

    @pl.when(t > 0)
    def _():
        sb_ref[...] = scores(1)
        _first_step(sc_ref[...], v_ref[0, 0:tk, :], m_ref, acc_ref)
        _online_step(meta_scores(), vm_ref[...], m_ref, acc_ref)

        def pair(j):
            sa_ref[...] = scores(2 * j + 2)
            softmax_pv(sb_ref, 2 * j + 1, False)
            sb_ref[...] = scores(2 * j + 3)
            softmax_pv(sa_ref, 2 * j + 2, False)

        def two_pairs(j, carry):
            pair(2 * j)
            pair(2 * j + 1)
            return carry

        n_pairs = (t - 1) // 2
        lax.fori_loop(0, n_pairs // 2, two_pairs, 0)

        @pl.when(n_pairs % 2 == 1)
        def _():
            pair(n_pairs - 1)

        kt = 2 * n_pairs + 1

        @pl.when(kt == t)
        def _():
            sc_ref[...] = next_scores0()
            softmax_pv(sb_ref, kt, True)

        @pl.when(kt < t)
        def _():
            sa_ref[...] = scores(kt + 1)
            softmax_pv(sb_ref, kt, False)
            sc_ref[...] = next_scores0()
            softmax_pv(sa_ref, kt + 1, True)


def _diff_attn_kernel(slopes_ref, q_ref, qn_ref, k_ref, v_ref, pk_ref, km_ref, vm_ref, pkm_ref,
                      g_ref, lam_ref, sw_ref, o_ref, qs_ref, sa_ref, sb_ref, sc_ref, m_ref,
                      acc_ref, *, tq, tk, lam_init):
    h = pl.program_id(1)
    t = pl.program_id(2)
    rows = 2 * tq

    @pl.when(t == 0)
    def _():
        slope = jnp.full((1, LANES), slopes_ref[h] * LOG2E, jnp.float32)
        s0 = slope.astype(jnp.bfloat16).astype(jnp.float32)
        s1 = (slope - s0).astype(jnp.bfloat16).astype(jnp.float32)
        s2 = (slope - s0 - s1).astype(jnp.bfloat16).astype(jnp.float32)
        ln = lax.broadcasted_iota(jnp.int32, (1, LANES), 1)
        piece = jnp.where(ln % 3 == 0, s0, jnp.where(ln % 3 == 1, s1, s2))
        q_pos = jnp.where(ln < 3, piece * float(POS_RADIX), jnp.where(ln < 6, piece, 0.0))
        qs_ref[:, LANES:2 * LANES] = jnp.broadcast_to(q_pos, (rows, LANES)).astype(qs_ref.dtype)

    def stacked(qb):
        lane = lax.broadcasted_iota(jnp.int32, (tq, LANES), 1)
        zero = jnp.zeros_like(qb)
        return jnp.concatenate([jnp.where(lane < A_QK_DIM, qb, zero),
                                jnp.where(lane >= A_QK_DIM, qb, zero)], axis=0)

    def tile_scores(qs, k0):
        kk = jnp.concatenate([k_ref[0, pl.ds(k0, tk), :], pk_ref[pl.ds(k0, tk), :]], axis=1)
        return lax.dot_general(qs, kk, _NT, preferred_element_type=jnp.float32)

    qs_ref[:, 0:LANES] = stacked(q_ref[0])

    def scores(kt):
        return tile_scores(qs_ref[...], pl.multiple_of(kt * tk, tk))

    def next_scores0():
        qn = jnp.concatenate([stacked(qn_ref[0]), qs_ref[:, LANES:2 * LANES]], axis=1)
        return tile_scores(qn, 0)

    def meta_scores():
        kk = jnp.concatenate([km_ref[...], pkm_ref[...]], axis=1)
        s = lax.dot_general(qs_ref[...], kk, _NT, preferred_element_type=jnp.float32)
        cm = lax.broadcasted_iota(jnp.int32, (rows, META_PAD), 1)
        return jnp.where(cm < N_META, s, -jnp.inf)

    def diag_mask():
        r = lax.broadcasted_iota(jnp.int32, (rows, tk), 0)
        r = jnp.where(r >= tq, r - tq, r)
        c = lax.broadcasted_iota(jnp.int32, (rows, tk), 1)
        return c <= r

    _causal_flash(t, tk, scores, next_scores0, meta_scores, diag_mask,
                  v_ref, vm_ref, sa_ref, sb_ref, sc_ref, m_ref, acc_ref)

    lp = lam_ref[...]
    lam = (jnp.exp(jnp.sum(lp[0:1] * lp[1:2], axis=-1, keepdims=True))
           - jnp.exp(jnp.sum(lp[2:3] * lp[3:4], axis=-1, keepdims=True)) + lam_init)
    o = acc_ref[:, 0:A_V_DIM] / acc_ref[:, A_V_DIM:A_V_DIM + LANES]
    o = o[0:tq] - lam * o[tq:rows]
    o = (o * lax.rsqrt(jnp.mean(o * o, axis=-1, keepdims=True) + RMS_EPS)) * sw_ref[...]
    o = o * (1.0 - lam_init)
    o_ref[0] = (o * _silu(g_ref[0])).astype(o_ref.dtype)


def _pos_columns(pos):
    a = (pos // POS_RADIX).astype(jnp.float32)[:, None]
    b = (pos % POS_RADIX).astype(jnp.float32)[:, None]
    ln = jnp.arange(LANES)[None, :]
    return jnp.where(ln < 3, a, jnp.where(ln < 6, b, 0.0)).astype(jnp.bfloat16)


def _diff_attn(slopes, qkv, kv_meta, gates, lam_p, subln_w, tq, tk, lam_init):
    b, s, _ = qkv.shape
    hq = A_HEADS
    pos_x = _pos_columns(jnp.arange(s, dtype=jnp.int32) + N_META)
    pos_m = _pos_columns(jnp.arange(META_PAD, dtype=jnp.int32))
    assert tq == tk
    last = s // tq - 1
    return pl.pallas_call(
        functools.partial(_diff_attn_kernel, tq=tq, tk=tk, lam_init=lam_init),
        grid=(b, hq, s // tq),
        in_specs=[
            pl.BlockSpec(memory_space=pltpu.SMEM),
            pl.BlockSpec((1, tq, LANES), lambda b, h, t: (b, t, h)),
            pl.BlockSpec((1, tq, LANES), lambda b, h, t: (b, jnp.minimum(t + 1, last), h)),
            pl.BlockSpec((1, s, LANES), lambda b, h, t: (b, 0, hq + h)),
            pl.BlockSpec((1, s, LANES), lambda b, h, t: (b, 0, 2 * hq + h)),
            pl.BlockSpec((s, LANES), lambda b, h, t: (0, 0)),
            pl.BlockSpec((META_PAD, LANES), lambda b, h, t: (0, hq + h)),
            pl.BlockSpec((META_PAD, LANES), lambda b, h, t: (0, 2 * hq + h)),
            pl.BlockSpec((META_PAD, LANES), lambda b, h, t: (0, 0)),
            pl.BlockSpec((1, tq, LANES), lambda b, h, t: (b, t, h)),
            pl.BlockSpec(lam_p.shape, lambda b, h, t: (0, 0)),
            pl.BlockSpec(subln_w.shape, lambda b, h, t: (0, 0)),
        ],
        out_specs=pl.BlockSpec((1, tq, LANES), lambda b, h, t: (b, t, h)),
        out_shape=jax.ShapeDtypeStruct((b, s, A_WIDTH), jnp.bfloat16),
        scratch_shapes=[pltpu.VMEM((2 * tq, 2 * LANES), jnp.bfloat16),
                        pltpu.VMEM((2 * tq, tk), jnp.float32),
                        pltpu.VMEM((2 * tq, tk), jnp.float32),
                        pltpu.VMEM((2 * tq, tk), jnp.float32),
                        pltpu.VMEM((2 * tq, LANES), jnp.float32),
                        pltpu.VMEM((2 * tq, A_V_DIM + LANES), jnp.float32)],
        compiler_params=_cparams(3),
        name="diff_attn",
    )(slopes, qkv, qkv, qkv, qkv, pos_x, kv_meta, kv_meta, pos_m, gates, lam_p, subln_w)


def _mla_attn_kernel(q_ref, qn_ref, k_ref, v_ref, km_ref, vm_ref, g_ref, o_ref,
                     sa_ref, sb_ref, sc_ref, m_ref, acc_ref, *, tq, tk):
    t = pl.program_id(2)

    def scores(kt):
        k0 = pl.multiple_of(kt * tk, tk)
        return lax.dot_general(q_ref[0], k_ref[0, pl.ds(k0, tk), :], _NT,
                               preferred_element_type=jnp.float32)

    def next_scores0():
        return lax.dot_general(qn_ref[0], k_ref[0, 0:tk, :], _NT,
                               preferred_element_type=jnp.float32)

    def meta_scores():
        s = lax.dot_general(q_ref[0], km_ref[...], _NT, preferred_element_type=jnp.float32)
        cm = lax.broadcasted_iota(jnp.int32, (tq, META_PAD), 1)
        return jnp.where(cm < N_META, s, -jnp.inf)

    def diag_mask():
        r = lax.broadcasted_iota(jnp.int32, (tq, tk), 0)
        c = lax.broadcasted_iota(jnp.int32, (tq, tk), 1)
        return c <= r

    _causal_flash(t, tk, scores, next_scores0, meta_scores, diag_mask,
                  v_ref, vm_ref, sa_ref, sb_ref, sc_ref, m_ref, acc_ref)

    o = acc_ref[:, 0:B_V_DIM] / acc_ref[:, B_V_DIM:B_V_DIM + LANES]
    o_ref[0] = (o * _silu(g_ref[0])).astype(o_ref.dtype)


def _mla_attn(q_cat, k_cat, v, km, vm, gates, tq, tk):
    b, s, _ = q_cat.shape
    g_blk0 = A_WIDTH // LANES
    assert tq == tk
    last = s // tq - 1
    return pl.pallas_call(
        functools.partial(_mla_attn_kernel, tq=tq, tk=tk),
        grid=(b, B_HEADS, s // tq),
        in_specs=[
            pl.BlockSpec((1, tq, B_QK_PAD), lambda b, h, t: (b, t, h)),
            pl.BlockSpec((1, tq, B_QK_PAD), lambda b, h, t: (b, jnp.minimum(t + 1, last), h)),
            pl.BlockSpec((1, s, B_QK_PAD), lambda b, h, t: (b, 0, h)),
            pl.BlockSpec((1, s, B_V_DIM), lambda b, h, t: (b, 0, h)),
            pl.BlockSpec((META_PAD, B_QK_PAD), lambda b, h, t: (0, h)),
            pl.BlockSpec((META_PAD, B_V_DIM), lambda b, h, t: (0, h)),
            pl.BlockSpec((1, tq, LANES), lambda b, h, t: (b, t, g_blk0 + h)),
        ],
        out_specs=pl.BlockSpec((1, tq, B_V_DIM), lambda b, h, t: (b, t, h)),
        out_shape=jax.ShapeDtypeStruct((b, s, B_WIDTH), jnp.bfloat16),
        scratch_shapes=[pltpu.VMEM((tq, tk), jnp.float32),
                        pltpu.VMEM((tq, tk), jnp.float32),
                        pltpu.VMEM((tq, tk), jnp.float32),
                        pltpu.VMEM((tq, LANES), jnp.float32),
                        pltpu.VMEM((tq, B_V_DIM + LANES), jnp.float32)],
        compiler_params=_cparams(3),
        name="mla_attn",
    )(q_cat, q_cat, k_cat, v, km, vm, gates)


def _out_proj_kernel(oa_ref, ob_ref, x_ref, wa_ref, wb_ref, fw_ref, y_ref):
    d = jnp.dot(oa_ref[...], wa_ref[...], preferred_element_type=jnp.float32)
    d = d + jnp.dot(ob_ref[...], wb_ref[...], preferred_element_type=jnp.float32)
    hres = x_ref[...] + d
    r = lax.rsqrt(jnp.mean(hres * hres, axis=-1, keepdims=True) + RMS_EPS)
    y_ref[...] = (hres * r) * fw_ref[...]


def _out_proj(oa, ob, x, wa, wb, fw, tm):
    rows, d = x.shape
    full = lambda a: pl.BlockSpec(a.shape, lambda i: (0, 0))
    return pl.pallas_call(
        _out_proj_kernel,
        grid=(rows // tm,),
        in_specs=[pl.BlockSpec((tm, oa.shape[1]), lambda i: (i, 0)),
                  pl.BlockSpec((tm, ob.shape[1]), lambda i: (i, 0)),
                  pl.BlockSpec((tm, d), lambda i: (i, 0)),
                  full(wa), full(wb), full(fw)],
        out_specs=pl.BlockSpec((tm, d), lambda i: (i, 0)),
        out_shape=jax.ShapeDtypeStruct((rows, d), jnp.float32),
        compiler_params=_cparams(1),
        name="out_proj",
    )(oa, ob, x, wa, wb, fw)


def _rot_half_cols(w):
    half = w.shape[-1] // 2
    return jnp.concatenate([-w[..., half:], w[..., :half]], axis=-1)


def _pad_cols(w, width):
    return jnp.pad(w, ((0, 0), (0, width - w.shape[1])))


def kernel(x, meta_tokens, attn_norm_w, w_in, diff_lambda, diff_subln_w, mla_q_norm_w, w_uq,
           mla_kv_norm_w, w_ukv, w_out, final_norm_w):
    bsz, seq, d = x.shape
    bf16 = jnp.bfloat16
    l = 0
    lam_init = 0.8 - 0.6 * math.exp(-0.3 * l)

    wi = w_in[l]
    a_q, a_k, a_v, a_g, b_cq, b_ckv, b_kr, b_g = jnp.split(
        wi, [1024, 2048, 3072, 4096, 4608, 4864, 4928], axis=1)
    w_all = jnp.concatenate(
        [a_q * (A_QK_DIM ** -0.5 * LOG2E), a_k, a_v,
         a_g, b_g,
         b_cq, b_ckv, _pad_cols(b_kr, LANES), _pad_cols(_rot_half_cols(b_kr), LANES)],
        axis=1).astype(bf16)
    widths = (3 * A_WIDTH, A_WIDTH + B_WIDTH, B_Q_LORA + B_KV_LORA + 2 * LANES)

    wuq = w_uq[l].reshape(B_Q_LORA, B_HEADS, B_NOPE + B_ROPE)
    wq_main = jnp.pad(wuq, ((0, 0), (0, 0), (0, B_QK_PAD - B_NOPE - B_ROPE)))
    wq_main = wq_main.reshape(B_Q_LORA, B_HEADS * B_QK_PAD).astype(bf16)
    wq_rot = jnp.pad(_rot_half_cols(wuq[..., B_NOPE:]), ((0, 0), (0, 0), (0, LANES - B_ROPE)))
    wq_rot = wq_rot.reshape(B_Q_LORA, B_HEADS * LANES).astype(bf16)
    wukv = w_ukv[l].reshape(B_KV_LORA, B_HEADS, B_NOPE + B_V_DIM)
    wk = wukv[..., :B_NOPE].reshape(B_KV_LORA, B_HEADS * B_NOPE).astype(bf16)
    wv = wukv[..., B_NOPE:].reshape(B_KV_LORA, B_WIDTH).astype(bf16)
    wo = w_out[l].astype(bf16)

    half = B_ROPE // 2
    inv = ROPE_THETA ** (-jnp.arange(half, dtype=jnp.float32) / half)
    inv = jnp.tile(inv, LANES // half)[None, :]
    slopes = jnp.exp2(-8.0 * jnp.arange(1, A_HEADS + 1, dtype=jnp.float32) / A_HEADS)

    nw = attn_norm_w[l][None, :]
    qnw = mla_q_norm_w[l][None, :]
    kvnw = mla_kv_norm_w[l][None, :]

    x2 = x.reshape(bsz * seq, d)
    w_tiles = w_all.reshape(d, -1, IN_PROJ_TN).transpose(1, 0, 2)
    qkv, gates, small = _in_proj(x2, nw, w_tiles, widths, 1024)
    meta = meta_tokens.astype(x.dtype)
    qkv_m, _, small_m = _in_proj(meta, nw, w_tiles, widths, N_META)

    q_cat, k_cat, v_b = _mla_proj(small, qnw, kvnw, inv, wq_main, wq_rot, wk, wv, 512, seq, N_META)
    _, k_cat_m, v_b_m = _mla_proj(small_m, qnw, kvnw, inv, wq_main, wq_rot, wk, wv, N_META,
                                  N_META, 0)

    pad_meta = lambda a: jnp.pad(a, ((0, META_PAD - N_META), (0, 0)))

    o_a = _diff_attn(slopes, qkv.reshape(bsz, seq, -1), pad_meta(qkv_m),
                     gates.reshape(bsz, seq, -1), diff_lambda[l].astype(jnp.float32),
                     diff_subln_w[l][None, :], 512, 512, lam_init)
    o_b = _mla_attn(q_cat.reshape(bsz, seq, -1), k_cat.reshape(bsz, seq, -1),
                    v_b.reshape(bsz, seq, -1), pad_meta(k_cat_m), pad_meta(v_b_m),
                    gates.reshape(bsz, seq, -1), 1024, 1024)

    y = _out_proj(o_a.reshape(bsz * seq, -1), o_b.reshape(bsz * seq, -1), x2,
                  wo[:A_WIDTH], wo[A_WIDTH:], final_norm_w[None, :], 512)
    return y.reshape(bsz, seq, d)
```

```python
import functools
import math

import jax
import jax.numpy as jnp
from jax import lax
from jax.experimental import pallas as pl
from jax.experimental.pallas import tpu as pltpu

N_META = 16
RMS_EPS = 1e-6
ROPE_THETA = 10000.0
LOG2E = math.log2(math.e)

A_HEADS = 8
A_QK_DIM = 64
A_V_DIM = 128
A_WIDTH = A_HEADS * A_V_DIM
B_HEADS = 8
B_Q_LORA = 512
B_KV_LORA = 256
B_NOPE = 128
B_ROPE = 64
B_V_DIM = 128
B_WIDTH = B_HEADS * B_V_DIM
B_QK_PAD = 256

LANES = 128
META_PAD = 128
IN_PROJ_TN = 512
POS_RADIX = 256
VMEM_LIMIT = 48 * 1024 * 1024

_NT = (((1,), (1,)), ((), ()))


def _cparams(n_grid):
    return pltpu.CompilerParams(
        dimension_semantics=("arbitrary",) * n_grid, vmem_limit_bytes=VMEM_LIMIT)


def _in_proj_kernel(x_ref, nw_ref, w_ref, qkv_ref, gates_ref, small_ref, xn_ref,
                    *, n_qkv, n_gates):
    j = pl.program_id(1)

    @pl.when(j == 0)
    def _():
        x = x_ref[...]
        r = lax.rsqrt(jnp.mean(x * x, axis=-1, keepdims=True) + RMS_EPS)
        xn_ref[...] = ((x * r) * nw_ref[...]).astype(xn_ref.dtype)

    def tile():
        return jnp.dot(xn_ref[...], w_ref[0], preferred_element_type=jnp.float32)

    @pl.when(j < n_qkv)
    def _():
        qkv_ref[...] = tile().astype(qkv_ref.dtype)

    @pl.when(jnp.logical_and(j >= n_qkv, j < n_qkv + n_gates))
    def _():
        gates_ref[...] = tile()

    @pl.when(j >= n_qkv + n_gates)
    def _():
        small_ref[...] = tile()


def _in_proj(x, nw, w_tiles, widths, tm):
    rows, k = x.shape
    tn = w_tiles.shape[2]
    n_qkv, n_gates, n_small = (w // tn for w in widths)
    assert (n_qkv * tn, n_gates * tn, n_small * tn) == tuple(widths)
    return pl.pallas_call(
        functools.partial(_in_proj_kernel, n_qkv=n_qkv, n_gates=n_gates),
        grid=(rows // tm, n_qkv + n_gates + n_small),
        in_specs=[
            pl.BlockSpec((tm, k), lambda i, j: (i, 0)),
            pl.BlockSpec((1, k), lambda i, j: (0, 0)),
            pl.BlockSpec((1, k, tn), lambda i, j: (j, 0, 0)),
        ],
        out_specs=[
            pl.BlockSpec((tm, tn), lambda i, j: (i, jnp.minimum(j, n_qkv - 1))),
            pl.BlockSpec((tm, tn), lambda i, j: (i, jnp.clip(j - n_qkv, 0, n_gates - 1))),
            pl.BlockSpec((tm, tn),
                         lambda i, j: (i, jnp.clip(j - n_qkv - n_gates, 0, n_small - 1))),
        ],
        out_shape=[jax.ShapeDtypeStruct((rows, widths[0]), jnp.bfloat16),
                   jax.ShapeDtypeStruct((rows, widths[1]), jnp.float32),
                   jax.ShapeDtypeStruct((rows, widths[2]), jnp.float32)],
        scratch_shapes=[pltpu.VMEM((tm, k), jnp.bfloat16)],
        compiler_params=_cparams(2),
        name="in_proj",
    )(x, nw, w_tiles)


def _mla_proj_kernel(small_ref, qnw_ref, kvnw_ref, inv_ref, wq_ref, wqr_ref, wk_ref, wv_ref,
                     q_ref, k_ref, v_ref, *, tm, seq, pos0, scale):
    def rms(x, w):
        r = lax.rsqrt(jnp.mean(x * x, axis=-1, keepdims=True) + RMS_EPS)
        return (x * r) * w

    cq = rms(small_ref[:, 0:B_Q_LORA], qnw_ref[...]).astype(jnp.bfloat16)
    ckv = rms(small_ref[:, B_Q_LORA:B_Q_LORA + B_KV_LORA], kvnw_ref[...]).astype(jnp.bfloat16)
    kr = small_ref[:, 768:896]
    kr_rot = small_ref[:, 896:1024]

    row = lax.broadcasted_iota(jnp.int32, (tm, LANES), 0)
    pos = (row + ((pl.program_id(0) * tm) % seq + pos0)).astype(jnp.float32)
    ang = pos * inv_ref[...]
    cos, sin = jnp.cos(ang), jnp.sin(ang)

    q1 = jnp.dot(cq, wq_ref[...], preferred_element_type=jnp.float32)
    q2 = jnp.dot(cq, wqr_ref[...], preferred_element_type=jnp.float32)
    kn = jnp.dot(ckv, wk_ref[...], preferred_element_type=jnp.float32)
    v_ref[...] = jnp.dot(ckv, wv_ref[...], preferred_element_type=jnp.float32).astype(v_ref.dtype)
    kpe = (kr * cos + kr_rot * sin).astype(k_ref.dtype)
    for h in range(B_HEADS):
        c0 = h * B_QK_PAD
        q_ref[:, c0:c0 + LANES] = (q1[:, c0:c0 + LANES] * scale).astype(q_ref.dtype)
        qpe = q1[:, c0 + LANES:c0 + 2 * LANES] * cos + q2[:, h * LANES:(h + 1) * LANES] * sin
        q_ref[:, c0 + LANES:c0 + 2 * LANES] = (qpe * scale).astype(q_ref.dtype)
        k_ref[:, c0:c0 + LANES] = kn[:, h * LANES:(h + 1) * LANES].astype(k_ref.dtype)
        k_ref[:, c0 + LANES:c0 + 2 * LANES] = kpe


def _mla_proj(small, qnw, kvnw, inv, wq, wqr, wk, wv, tm, seq, pos0):
    rows = small.shape[0]
    full = lambda a: pl.BlockSpec(a.shape, lambda i: (0, 0))
    scale = (B_NOPE + B_ROPE) ** -0.5 * LOG2E
    return pl.pallas_call(
        functools.partial(_mla_proj_kernel, tm=tm, seq=seq, pos0=pos0, scale=scale),
        grid=(rows // tm,),
        in_specs=[pl.BlockSpec((tm, small.shape[1]), lambda i: (i, 0)),
                  full(qnw), full(kvnw), full(inv), full(wq), full(wqr), full(wk), full(wv)],
        out_specs=[pl.BlockSpec((tm, B_HEADS * B_QK_PAD), lambda i: (i, 0)),
                   pl.BlockSpec((tm, B_HEADS * B_QK_PAD), lambda i: (i, 0)),
                   pl.BlockSpec((tm, B_WIDTH), lambda i: (i, 0))],
        out_shape=[jax.ShapeDtypeStruct((rows, B_HEADS * B_QK_PAD), jnp.bfloat16),
                   jax.ShapeDtypeStruct((rows, B_HEADS * B_QK_PAD), jnp.bfloat16),
                   jax.ShapeDtypeStruct((rows, B_WIDTH), jnp.bfloat16)],
        compiler_params=_cparams(1),
        name="mla_proj",
    )(small, qnw, kvnw, inv, wq, wqr, wk, wv)


def _silu(g):
    return g * (1.0 / (1.0 + jnp.exp(-g)))


def _lanes(x, width):
    return jnp.tile(x, (1, width // LANES))


def _with_ones(v):
    return jnp.concatenate([v, jnp.ones((v.shape[0], LANES), v.dtype)], axis=1)


def _online_step(s, v, m_ref, acc_ref, finish=None):
    tk = s.shape[-1]
    m_prev = m_ref[...]
    m_new = jnp.maximum(m_prev, jnp.max(s, axis=-1, keepdims=True))
    alpha = jnp.exp2(m_prev - m_new)
    p = jnp.exp2(s - _lanes(m_new, tk))
    pv = jnp.dot(p.astype(v.dtype), _with_ones(v), preferred_element_type=jnp.float32)
    acc = _lanes(alpha, acc_ref.shape[-1]) * acc_ref[...] + pv
    if finish is None:
        acc_ref[...] = acc
        m_ref[...] = m_new
    else:
        finish(acc)


def _first_step(s, v, m_ref, acc_ref):
    tk = s.shape[-1]
    m_new = jnp.broadcast_to(jnp.max(s, axis=-1, keepdims=True), m_ref.shape)
    p = jnp.exp2(s - _lanes(m_new, tk))
    acc_ref[...] = jnp.dot(p.astype(v.dtype), _with_ones(v), preferred_element_type=jnp.float32)
    m_ref[...] = m_new


def _causal_flash(t, tk, scores, next_scores0, meta_scores, diag_mask, finish,
                  v_ref, vm_ref, sa_ref, sb_ref, sc_ref, m_ref, acc_ref):
    def softmax_pv(s_ref, kt, masked):
        k0 = pl.multiple_of(kt * tk, tk)
        s = s_ref[...]
        if masked:
            s = jnp.where(diag_mask(), s, -jnp.inf)
        _online_step(s, v_ref[0, pl.ds(k0, tk), :], m_ref, acc_ref,
                     finish if masked else None)

    @pl.when(t == 0)
    def _():
        sc_ref[...] = scores(0)
        _first_step(meta_scores(), vm_ref[...], m_ref, acc_ref)
        softmax_pv(sc_ref, 0, True)
        sc_ref[...] = next_scores0()

    @pl.when(t > 0)
    def _():
        sb_ref[...] = scores(1)
        _first_step(sc_ref[...], v_ref[0, 0:tk, :], m_ref, acc_ref)
        _online_step(meta_scores(), vm_ref[...], m_ref, acc_ref)

        def pair(j):
            sa_ref[...] = scores(2 * j + 2)
            softmax_pv(sb_ref, 2 * j + 1, False)
            sb_ref[...] = scores(2 * j + 3)
            softmax_pv(sa_ref, 2 * j + 2, False)

        def two_pairs(j, carry):
            pair(2 * j)
            pair(2 * j + 1)
            return carry

        n_pairs = (t - 1) // 2
        lax.fori_loop(0, n_pairs // 2, two_pairs, 0)

        @pl.when(n_pairs % 2 == 1)
        def _():
            pair(n_pairs - 1)

        kt = 2 * n_pairs + 1

        @pl.when(kt == t)
        def _():
            sc_ref[...] = next_scores0()
            softmax_pv(sb_ref, kt, True)

        @pl.when(kt < t)
        def _():
            sa_ref[...] = scores(kt + 1)
            softmax_pv(sb_ref, kt, False)
            sc_ref[...] = next_scores0()
            softmax_pv(sa_ref, kt + 1, True)


def _diff_attn_kernel(slopes_ref, q_ref, qn_ref, k_ref, v_ref, pk_ref, km_ref, vm_ref, pkm_ref,
                      g_ref, lam_ref, sw_ref, o_ref, qs_ref, sa_ref, sb_ref, sc_ref, m_ref,
                      acc_ref, *, tq, tk, lam_init):
    h = pl.program_id(1)
    t = pl.program_id(2)
    rows = 2 * tq

    @pl.when(t == 0)
    def _():
        slope = jnp.full((1, LANES), slopes_ref[h] * LOG2E, jnp.float32)
        s0 = slope.astype(jnp.bfloat16).astype(jnp.float32)
        s1 = (slope - s0).astype(jnp.bfloat16).astype(jnp.float32)
        s2 = (slope - s0 - s1).astype(jnp.bfloat16).astype(jnp.float32)
        ln = lax.broadcasted_iota(jnp.int32, (1, LANES), 1)
        piece = jnp.where(ln % 3 == 0, s0, jnp.where(ln % 3 == 1, s1, s2))
        q_pos = jnp.where(ln < 3, piece * float(POS_RADIX), jnp.where(ln < 6, piece, 0.0))
        qs_ref[:, LANES:2 * LANES] = jnp.broadcast_to(q_pos, (rows, LANES)).astype(qs_ref.dtype)

    def stacked(qb):
        lane = lax.broadcasted_iota(jnp.int32, (tq, LANES), 1)
        zero = jnp.zeros_like(qb)
        return jnp.concatenate([jnp.where(lane < A_QK_DIM, qb, zero),
                                jnp.where(lane >= A_QK_DIM, qb, zero)], axis=0)

    def tile_scores(qs, k0):
        kk = jnp.concatenate([k_ref[0, pl.ds(k0, tk), :], pk_ref[pl.ds(k0, tk), :]], axis=1)
        return lax.dot_general(qs, kk, _NT, preferred_element_type=jnp.float32)

    qs_ref[:, 0:LANES] = stacked(q_ref[0])

    def scores(kt):
        return tile_scores(qs_ref[...], pl.multiple_of(kt * tk, tk))

    def next_scores0():
        qn = jnp.concatenate([stacked(qn_ref[0]), qs_ref[:, LANES:2 * LANES]], axis=1)
        return tile_scores(qn, 0)

    def meta_scores():
        kk = jnp.concatenate([km_ref[...], pkm_ref[...]], axis=1)
        s = lax.dot_general(qs_ref[...], kk, _NT, preferred_element_type=jnp.float32)
        cm = lax.broadcasted_iota(jnp.int32, (rows, META_PAD), 1)
        return jnp.where(cm < N_META, s, -jnp.inf)

    def diag_mask():
        r = lax.broadcasted_iota(jnp.int32, (tq, tk), 0)
        c = lax.broadcasted_iota(jnp.int32, (tq, tk), 1)
        return jnp.concatenate([c <= r, c <= r], axis=0)

    def finish(acc):
        lp = lam_ref[...]
        lam = (jnp.exp(jnp.sum(lp[0:1] * lp[1:2], axis=-1, keepdims=True))
               - jnp.exp(jnp.sum(lp[2:3] * lp[3:4], axis=-1, keepdims=True)) + lam_init)
        o = acc[:, 0:A_V_DIM] / acc[:, A_V_DIM:A_V_DIM + LANES]
        o = o[0:tq] - lam * o[tq:rows]
        o = (o * lax.rsqrt(jnp.mean(o * o, axis=-1, keepdims=True) + RMS_EPS)) * sw_ref[...]
        o = o * (1.0 - lam_init)
        o_ref[0] = (o * _silu(g_ref[0])).astype(o_ref.dtype)

    _causal_flash(t, tk, scores, next_scores0, meta_scores, diag_mask, finish,
                  v_ref, vm_ref, sa_ref, sb_ref, sc_ref, m_ref, acc_ref)


def _pos_columns(pos):
    a = (pos // POS_RADIX).astype(jnp.float32)[:, None]
    b = (pos % POS_RADIX).astype(jnp.float32)[:, None]
    ln = jnp.arange(LANES)[None, :]
    return jnp.where(ln < 3, a, jnp.where(ln < 6, b, 0.0)).astype(jnp.bfloat16)


def _diff_attn(slopes, qkv, kv_meta, gates, lam_p, subln_w, tq, tk, lam_init):
    b, s, _ = qkv.shape
    hq = A_HEADS
    pos_x = _pos_columns(jnp.arange(s, dtype=jnp.int32) + N_META)
    pos_m = _pos_columns(jnp.arange(META_PAD, dtype=jnp.int32))
    assert tq == tk
    last = s // tq - 1
    return pl.pallas_call(
        functools.partial(_diff_attn_kernel, tq=tq, tk=tk, lam_init=lam_init),
        grid=(b, hq, s // tq),
        in_specs=[
            pl.BlockSpec(memory_space=pltpu.SMEM),
            pl.BlockSpec((1, tq, LANES), lambda b, h, t: (b, t, h)),
            pl.BlockSpec((1, tq, LANES), lambda b, h, t: (b, jnp.minimum(t + 1, last), h)),
            pl.BlockSpec((1, s, LANES), lambda b, h, t: (b, 0, hq + h)),
            pl.BlockSpec((1, s, LANES), lambda b, h, t: (b, 0, 2 * hq + h)),
            pl.BlockSpec((s, LANES), lambda b, h, t: (0, 0)),
            pl.BlockSpec((META_PAD, LANES), lambda b, h, t: (0, hq + h)),
            pl.BlockSpec((META_PAD, LANES), lambda b, h, t: (0, 2 * hq + h)),
            pl.BlockSpec((META_PAD, LANES), lambda b, h, t: (0, 0)),
            pl.BlockSpec((1, tq, LANES), lambda b, h, t: (b, t, h)),
            pl.BlockSpec(lam_p.shape, lambda b, h, t: (0, 0)),
            pl.BlockSpec(subln_w.shape, lambda b, h, t: (0, 0)),
        ],
        out_specs=pl.BlockSpec((1, tq, LANES), lambda b, h, t: (b, t, h)),
        out_shape=jax.ShapeDtypeStruct((b, s, A_WIDTH), jnp.bfloat16),
        scratch_shapes=[pltpu.VMEM((2 * tq, 2 * LANES), jnp.bfloat16),
                        pltpu.VMEM((2 * tq, tk), jnp.float32),
                        pltpu.VMEM((2 * tq, tk), jnp.float32),
                        pltpu.VMEM((2 * tq, tk), jnp.float32),
                        pltpu.VMEM((2 * tq, LANES), jnp.float32),
                        pltpu.VMEM((2 * tq, A_V_DIM + LANES), jnp.float32)],
        compiler_params=_cparams(3),
        name="diff_attn",
    )(slopes, qkv, qkv, qkv, qkv, pos_x, kv_meta, kv_meta, pos_m, gates, lam_p, subln_w)


def _mla_attn_kernel(q_ref, qn_ref, k_ref, v_ref, km_ref, vm_ref, g_ref, o_ref,
                     sa_ref, sb_ref, sc_ref, m_ref, acc_ref, *, tq, tk):
    t = pl.program_id(2)

    def scores(kt):
        k0 = pl.multiple_of(kt * tk, tk)
        return lax.dot_general(q_ref[0], k_ref[0, pl.ds(k0, tk), :], _NT,
                               preferred_element_type=jnp.float32)

    def next_scores0():
        return lax.dot_general(qn_ref[0], k_ref[0, 0:tk, :], _NT,
                               preferred_element_type=jnp.float32)

    def meta_scores():
        s = lax.dot_general(q_ref[0], km_ref[...], _NT, preferred_element_type=jnp.float32)
        cm = lax.broadcasted_iota(jnp.int32, (tq, META_PAD), 1)
        return jnp.where(cm < N_META, s, -jnp.inf)

    def diag_mask():
        r = lax.broadcasted_iota(jnp.int32, (tq, tk), 0)
        c = lax.broadcasted_iota(jnp.int32, (tq, tk), 1)
        return c <= r

    def finish(acc):
        o = acc[:, 0:B_V_DIM] / acc[:, B_V_DIM:B_V_DIM + LANES]
        o_ref[0] = (o * _silu(g_ref[0])).astype(o_ref.dtype)

    _causal_flash(t, tk, scores, next_scores0, meta_scores, diag_mask, finish,
                  v_ref, vm_ref, sa_ref, sb_ref, sc_ref, m_ref, acc_ref)


def _mla_attn(q_cat, k_cat, v, km, vm, gates, tq, tk):
    b, s, _ = q_cat.shape
    g_blk0 = A_WIDTH // LANES
    assert tq == tk
    last = s // tq - 1
    return pl.pallas_call(
        functools.partial(_mla_attn_kernel, tq=tq, tk=tk),
        grid=(b, B_HEADS, s // tq),
        in_specs=[
            pl.BlockSpec((1, tq, B_QK_PAD), lambda b, h, t: (b, t, h)),
            pl.BlockSpec((1, tq, B_QK_PAD), lambda b, h, t: (b, jnp.minimum(t + 1, last), h)),
            pl.BlockSpec((1, s, B_QK_PAD), lambda b, h, t: (b, 0, h)),
            pl.BlockSpec((1, s, B_V_DIM), lambda b, h, t: (b, 0, h)),
            pl.BlockSpec((META_PAD, B_QK_PAD), lambda b, h, t: (0, h)),
            pl.BlockSpec((META_PAD, B_V_DIM), lambda b, h, t: (0, h)),
            pl.BlockSpec((1, tq, LANES), lambda b, h, t: (b, t, g_blk0 + h)),
        ],
        out_specs=pl.BlockSpec((1, tq, B_V_DIM), lambda b, h, t: (b, t, h)),
        out_shape=jax.ShapeDtypeStruct((b, s, B_WIDTH), jnp.bfloat16),
        scratch_shapes=[pltpu.VMEM((tq, tk), jnp.float32),
                        pltpu.VMEM((tq, tk), jnp.float32),
                        pltpu.VMEM((tq, tk), jnp.float32),
                        pltpu.VMEM((tq, LANES), jnp.float32),
                        pltpu.VMEM((tq, B_V_DIM + LANES), jnp.float32)],
        compiler_params=_cparams(3),
        name="mla_attn",
    )(q_cat, q_cat, k_cat, v, km, vm, gates)


def _out_proj_kernel(oa_ref, ob_ref, x_ref, wa_ref, wb_ref, fw_ref, y_ref):
    d = jnp.dot(oa_ref[...], wa_ref[...], preferred_element_type=jnp.float32)
    d = d + jnp.dot(ob_ref[...], wb_ref[...], preferred_element_type=jnp.float32)
    hres = x_ref[...] + d
    r = lax.rsqrt(jnp.mean(hres * hres, axis=-1, keepdims=True) + RMS_EPS)
    y_ref[...] = (hres * r) * fw_ref[...]


def _out_proj(oa, ob, x, wa, wb, fw, tm):
    rows, d = x.shape
    full = lambda a: pl.BlockSpec(a.shape, lambda i: (0, 0))
    return pl.pallas_call(
        _out_proj_kernel,
        grid=(rows // tm,),
        in_specs=[pl.BlockSpec((tm, oa.shape[1]), lambda i: (i, 0)),
                  pl.BlockSpec((tm, ob.shape[1]), lambda i: (i, 0)),
                  pl.BlockSpec((tm, d), lambda i: (i, 0)),
                  full(wa), full(wb), full(fw)],
        out_specs=pl.BlockSpec((tm, d), lambda i: (i, 0)),
        out_shape=jax.ShapeDtypeStruct((rows, d), jnp.float32),
        compiler_params=_cparams(1),
        name="out_proj",
    )(oa, ob, x, wa, wb, fw)


def _rot_half_cols(w):
    half = w.shape[-1] // 2
    return jnp.concatenate([-w[..., half:], w[..., :half]], axis=-1)


def _pad_cols(w, width):
    return jnp.pad(w, ((0, 0), (0, width - w.shape[1])))


def kernel(x, meta_tokens, attn_norm_w, w_in, diff_lambda, diff_subln_w, mla_q_norm_w, w_uq,
           mla_kv_norm_w, w_ukv, w_out, final_norm_w):
    bsz, seq, d = x.shape
    bf16 = jnp.bfloat16
    l = 0
    lam_init = 0.8 - 0.6 * math.exp(-0.3 * l)

    wi = w_in[l]
    a_q, a_k, a_v, a_g, b_cq, b_ckv, b_kr, b_g = jnp.split(
        wi, [1024, 2048, 3072, 4096, 4608, 4864, 4928], axis=1)
    w_all = jnp.concatenate(
        [a_q * (A_QK_DIM ** -0.5 * LOG2E), a_k, a_v,
         a_g, b_g,
         b_cq, b_ckv, _pad_cols(b_kr, LANES), _pad_cols(_rot_half_cols(b_kr), LANES)],
        axis=1).astype(bf16)
    widths = (3 * A_WIDTH, A_WIDTH + B_WIDTH, B_Q_LORA + B_KV_LORA + 2 * LANES)

    wuq = w_uq[l].reshape(B_Q_LORA, B_HEADS, B_NOPE + B_ROPE)
    wq_main = jnp.pad(wuq, ((0, 0), (0, 0), (0, B_QK_PAD - B_NOPE - B_ROPE)))
    wq_main = wq_main.reshape(B_Q_LORA, B_HEADS * B_QK_PAD).astype(bf16)
    wq_rot = jnp.pad(_rot_half_cols(wuq[..., B_NOPE:]), ((0, 0), (0, 0), (0, LANES - B_ROPE)))
    wq_rot = wq_rot.reshape(B_Q_LORA, B_HEADS * LANES).astype(bf16)
    wukv = w_ukv[l].reshape(B_KV_LORA, B_HEADS, B_NOPE + B_V_DIM)
    wk = wukv[..., :B_NOPE].reshape(B_KV_LORA, B_HEADS * B_NOPE).astype(bf16)
    wv = wukv[..., B_NOPE:].reshape(B_KV_LORA, B_WIDTH).astype(bf16)
    wo = w_out[l].astype(bf16)

    half = B_ROPE // 2
    inv = ROPE_THETA ** (-jnp.arange(half, dtype=jnp.float32) / half)
    inv = jnp.tile(inv, LANES // half)[None, :]
    slopes = jnp.exp2(-8.0 * jnp.arange(1, A_HEADS + 1, dtype=jnp.float32) / A_HEADS)

    nw = attn_norm_w[l][None, :]
    qnw = mla_q_norm_w[l][None, :]
    kvnw = mla_kv_norm_w[l][None, :]

    x2 = x.reshape(bsz * seq, d)
    w_tiles = w_all.reshape(d, -1, IN_PROJ_TN).transpose(1, 0, 2)
    qkv, gates, small = _in_proj(x2, nw, w_tiles, widths, 1024)
    meta = meta_tokens.astype(x.dtype)
    qkv_m, _, small_m = _in_proj(meta, nw, w_tiles, widths, N_META)

    q_cat, k_cat, v_b = _mla_proj(small, qnw, kvnw, inv, wq_main, wq_rot, wk, wv, 512, seq, N_META)
    _, k_cat_m, v_b_m = _mla_proj(small_m, qnw, kvnw, inv, wq_main, wq_rot, wk, wv, N_META,
                                  N_META, 0)

    pad_meta = lambda a: jnp.pad(a, ((0, META_PAD - N_META), (0, 0)))

    o_a = _diff_attn(slopes, qkv.reshape(bsz, seq, -1), pad_meta(qkv_m),
                     gates.reshape(bsz, seq, -1), diff_lambda[l].astype(jnp.float32),
                     diff_subln_w[l][None, :], 512, 512, lam_init)
    o_b = _mla_attn(q_cat.reshape(bsz, seq, -1), k_cat.reshape(bsz, seq, -1),
                    v_b.reshape(bsz, seq, -1), pad_meta(k_cat_m), pad_meta(v_b_m),
                    gates.reshape(bsz, seq, -1), 1024, 1024)

    y = _out_proj(o_a.reshape(bsz * seq, -1), o_b.reshape(bsz * seq, -1), x2,
                  wo[:A_WIDTH], wo[A_WIDTH:], final_norm_w[None, :], 512)
    return y.reshape(bsz, seq, d)
```

```python
import functools
import math

import jax
import jax.numpy as jnp
from jax import lax
from jax.experimental import pallas as pl
from jax.experimental.pallas import tpu as pltpu

N_META = 16
RMS_EPS = 1e-6
ROPE_THETA = 10000.0
LOG2E = math.log2(math.e)

A_HEADS = 8
A_QK_DIM = 64
A_V_DIM = 128
A_WIDTH = A_HEADS * A_V_DIM
B_HEADS = 8
B_Q_LORA = 512
B_KV_LORA = 256
B_NOPE = 128
B_ROPE = 64
B_V_DIM = 128
B_WIDTH = B_HEADS * B_V_DIM
B_QK_PAD = 256

LANES = 128
META_PAD = 128
IN_PROJ_TN = 512
POS_RADIX = 256
VMEM_LIMIT = 48 * 1024 * 1024
IN_PROJ_VMEM_LIMIT = 56 * 1024 * 1024

_NT = (((1,), (1,)), ((), ()))


def _cparams(n_grid, vmem_limit=VMEM_LIMIT):
    return pltpu.CompilerParams(
        dimension_semantics=("arbitrary",) * n_grid, vmem_limit_bytes=vmem_limit)


def _in_proj_kernel(x_ref, nw_ref, w_ref, qkv_ref, gates_ref, small_ref, xn_ref, *, tn):
    x = x_ref[...]
    r = lax.rsqrt(jnp.mean(x * x, axis=-1, keepdims=True) + RMS_EPS)
    xn_ref[...] = ((x * r) * nw_ref[...]).astype(xn_ref.dtype)
    col = 0
    for out_ref in (qkv_ref, gates_ref, small_ref):
        for c in range(0, out_ref.shape[1], tn):
            out_ref[:, c:c + tn] = jnp.dot(
                xn_ref[...], w_ref[:, col:col + tn],
                preferred_element_type=jnp.float32).astype(out_ref.dtype)
            col += tn


def _in_proj(x, nw, w_all, widths, tm):
    rows, k = x.shape
    assert sum(widths) == w_all.shape[1] and all(w % IN_PROJ_TN == 0 for w in widths)
    return pl.pallas_call(
        functools.partial(_in_proj_kernel, tn=IN_PROJ_TN),
        grid=(rows // tm,),
        in_specs=[
            pl.BlockSpec((tm, k), lambda i: (i, 0)),
            pl.BlockSpec((1, k), lambda i: (0, 0)),
            pl.BlockSpec(w_all.shape, lambda i: (0, 0), pipeline_mode=pl.Buffered(1)),
        ],
        out_specs=[pl.BlockSpec((tm, w), lambda i: (i, 0)) for w in widths],
        out_shape=[jax.ShapeDtypeStruct((rows, widths[0]), jnp.bfloat16),
                   jax.ShapeDtypeStruct((rows, widths[1]), jnp.float32),
                   jax.ShapeDtypeStruct((rows, widths[2]), jnp.float32)],
        scratch_shapes=[pltpu.VMEM((tm, k), jnp.bfloat16)],
        compiler_params=_cparams(1, IN_PROJ_VMEM_LIMIT),
        name="in_proj",
    )(x, nw, w_all)


def _mla_proj_kernel(small_ref, qnw_ref, kvnw_ref, inv_ref, wq_ref, wqr_ref, wk_ref, wv_ref,
                     q_ref, k_ref, v_ref, *, tm, seq, pos0, scale):
    def rms(x, w):
        r = lax.rsqrt(jnp.mean(x * x, axis=-1, keepdims=True) + RMS_EPS)
        return (x * r) * w

    cq = rms(small_ref[:, 0:B_Q_LORA], qnw_ref[...]).astype(jnp.bfloat16)
    ckv = rms(small_ref[:, B_Q_LORA:B_Q_LORA + B_KV_LORA], kvnw_ref[...]).astype(jnp.bfloat16)
    kr = small_ref[:, 768:896]
    kr_rot = small_ref[:, 896:1024]

    row = lax.broadcasted_iota(jnp.int32, (tm, LANES), 0)
    pos = (row + ((pl.program_id(0) * tm) % seq + pos0)).astype(jnp.float32)
    ang = pos * inv_ref[...]
    cos, sin = jnp.cos(ang), jnp.sin(ang)

    q1 = jnp.dot(cq, wq_ref[...], preferred_element_type=jnp.float32)
    q2 = jnp.dot(cq, wqr_ref[...], preferred_element_type=jnp.float32)
    kn = jnp.dot(ckv, wk_ref[...], preferred_element_type=jnp.float32)
    v_ref[...] = jnp.dot(ckv, wv_ref[...], preferred_element_type=jnp.float32).astype(v_ref.dtype)
    kpe = (kr * cos + kr_rot * sin).astype(k_ref.dtype)
    for h in range(B_HEADS):
        c0 = h * B_QK_PAD
        q_ref[:, c0:c0 + LANES] = (q1[:, c0:c0 + LANES] * scale).astype(q_ref.dtype)
        qpe = q1[:, c0 + LANES:c0 + 2 * LANES] * cos + q2[:, h * LANES:(h + 1) * LANES] * sin
        q_ref[:, c0 + LANES:c0 + 2 * LANES] = (qpe * scale).astype(q_ref.dtype)
        k_ref[:, c0:c0 + LANES] = kn[:, h * LANES:(h + 1) * LANES].astype(k_ref.dtype)
        k_ref[:, c0 + LANES:c0 + 2 * LANES] = kpe


def _mla_proj(small, qnw, kvnw, inv, wq, wqr, wk, wv, tm, seq, pos0):
    rows = small.shape[0]
    full = lambda a: pl.BlockSpec(a.shape, lambda i: (0, 0))
    scale = (B_NOPE + B_ROPE) ** -0.5 * LOG2E
    return pl.pallas_call(
        functools.partial(_mla_proj_kernel, tm=tm, seq=seq, pos0=pos0, scale=scale),
        grid=(rows // tm,),
        in_specs=[pl.BlockSpec((tm, small.shape[1]), lambda i: (i, 0)),
                  full(qnw), full(kvnw), full(inv), full(wq), full(wqr), full(wk), full(wv)],
        out_specs=[pl.BlockSpec((tm, B_HEADS * B_QK_PAD), lambda i: (i, 0)),
                   pl.BlockSpec((tm, B_HEADS * B_QK_PAD), lambda i: (i, 0)),
                   pl.BlockSpec((tm, B_WIDTH), lambda i: (i, 0))],
        out_shape=[jax.ShapeDtypeStruct((rows, B_HEADS * B_QK_PAD), jnp.bfloat16),
                   jax.ShapeDtypeStruct((rows, B_HEADS * B_QK_PAD), jnp.bfloat16),
                   jax.ShapeDtypeStruct((rows, B_WIDTH), jnp.bfloat16)],
        compiler_params=_cparams(1),
        name="mla_proj",
    )(small, qnw, kvnw, inv, wq, wqr, wk, wv)


def _silu(g):
    return g * (1.0 / (1.0 + jnp.exp(-g)))


def _lanes(x, width):
    return jnp.tile(x, (1, width // LANES))


def _with_ones(v):
    return jnp.concatenate([v, jnp.ones((v.shape[0], LANES), v.dtype)], axis=1)


def _online_step(s, v, m_ref, acc_ref, finish=None):
    tk = s.shape[-1]
    m_prev = m_ref[...]
    m_new = jnp.maximum(m_prev, jnp.max(s, axis=-1, keepdims=True))
    alpha = jnp.exp2(m_prev - m_new)
    p = jnp.exp2(s - _lanes(m_new, tk))
    pv = jnp.dot(p.astype(v.dtype), _with_ones(v), preferred_element_type=jnp.float32)
    acc = _lanes(alpha, acc_ref.shape[-1]) * acc_ref[...] + pv
    if finish is None:
        acc_ref[...] = acc
        m_ref[...] = m_new
    else:
        finish(acc)


def _first_step(s, v, m_ref, acc_ref):
    tk = s.shape[-1]
    m_new = jnp.broadcast_to(jnp.max(s, axis=-1, keepdims=True), m_ref.shape)
    p = jnp.exp2(s - _lanes(m_new, tk))
    acc_ref[...] = jnp.dot(p.astype(v.dtype), _with_ones(v), preferred_element_type=jnp.float32)
    m_ref[...] = m_new


def _causal_flash(t, tk, scores, next_scores0, meta_scores, diag_mask, finish,
                  v_ref, vm_ref, sa_ref, sb_ref, sc_ref, m_ref, acc_ref):
    def softmax_pv(s_ref, kt, masked):
        k0 = pl.multiple_of(kt * tk, tk)
        s = s_ref[...]
        if masked:
            s = jnp.where(diag_mask(), s, -jnp.inf)
        _online_step(s, v_ref[0, pl.ds(k0, tk), :], m_ref, acc_ref,
                     finish if masked else None)

    @pl.when(t == 0)
    def _():
        sc_ref[...] = scores(0)
        _first_step(meta_scores(), vm_ref[...], m_ref, acc_ref)
        softmax_pv(sc_ref, 0, True)
        sc_ref[...] = next_scores0()

    @pl.when(t > 0)
    def _():
        sb_ref[...] = scores(1)
        _first_step(sc_ref[...], v_ref[0, 0:tk, :], m_ref, acc_ref)
        _online_step(meta_scores(), vm_ref[...], m_ref, acc_ref)

        def pair(j):
            sa_ref[...] = scores(2 * j + 2)
            softmax_pv(sb_ref, 2 * j + 1, False)
            sb_ref[...] = scores(2 * j + 3)
            softmax_pv(sa_ref, 2 * j + 2, False)

        def two_pairs(j, carry):
            pair(2 * j)
            pair(2 * j + 1)
            return carry

        n_pairs = (t - 1) // 2
        lax.fori_loop(0, n_pairs // 2, two_pairs, 0)

        @pl.when(n_pairs % 2 == 1)
        def _():
            pair(n_pairs - 1)

        kt = 2 * n_pairs + 1

        @pl.when(kt == t)
        def _():
            sc_ref[...] = next_scores0()
            softmax_pv(sb_ref, kt, True)

        @pl.when(kt < t)
        def _():
            sa_ref[...] = scores(kt + 1)
            softmax_pv(sb_ref, kt, False)
            sc_ref[...] = next_scores0()
            softmax_pv(sa_ref, kt + 1, True)


def _diff_attn_kernel(slopes_ref, q_ref, qn_ref, k_ref, v_ref, pk_ref, km_ref, vm_ref, pkm_ref,
                      g_ref, lam_ref, sw_ref, o_ref, qs_ref, sa_ref, sb_ref, sc_ref, m_ref,
                      acc_ref, *, tq, tk, lam_init):
    h = pl.program_id(1)
    t = pl.program_id(2)
    rows = 2 * tq

    @pl.when(t == 0)
    def _():
        slope = jnp.full((1, LANES), slopes_ref[h] * LOG2E, jnp.float32)
        s0 = slope.astype(jnp.bfloat16).astype(jnp.float32)
        s1 = (slope - s0).astype(jnp.bfloat16).astype(jnp.float32)
        s2 = (slope - s0 - s1).astype(jnp.bfloat16).astype(jnp.float32)
        ln = lax.broadcasted_iota(jnp.int32, (1, LANES), 1)
        piece = jnp.where(ln % 3 == 0, s0, jnp.where(ln % 3 == 1, s1, s2))
        q_pos = jnp.where(ln < 3, piece * float(POS_RADIX), jnp.where(ln < 6, piece, 0.0))
        qs_ref[:, LANES:2 * LANES] = jnp.broadcast_to(q_pos, (rows, LANES)).astype(qs_ref.dtype)

    def stacked(qb):
        lane = lax.broadcasted_iota(jnp.int32, (tq, LANES), 1)
        zero = jnp.zeros_like(qb)
        return jnp.concatenate([jnp.where(lane < A_QK_DIM, qb, zero),
                                jnp.where(lane >= A_QK_DIM, qb, zero)], axis=0)

    def tile_scores(qs, k0):
        kk = jnp.concatenate([k_ref[0, pl.ds(k0, tk), :], pk_ref[pl.ds(k0, tk), :]], axis=1)
        return lax.dot_general(qs, kk, _NT, preferred_element_type=jnp.float32)

    qs_ref[:, 0:LANES] = stacked(q_ref[0])

    def scores(kt):
        return tile_scores(qs_ref[...], pl.multiple_of(kt * tk, tk))

    def next_scores0():
        qn = jnp.concatenate([stacked(qn_ref[0]), qs_ref[:, LANES:2 * LANES]], axis=1)
        return tile_scores(qn, 0)

    def meta_scores():
        kk = jnp.concatenate([km_ref[...], pkm_ref[...]], axis=1)
        s = lax.dot_general(qs_ref[...], kk, _NT, preferred_element_type=jnp.float32)
        cm = lax.broadcasted_iota(jnp.int32, (rows, META_PAD), 1)
        return jnp.where(cm < N_META, s, -jnp.inf)

    def diag_mask():
        r = lax.broadcasted_iota(jnp.int32, (tq, tk), 0)
        c = lax.broadcasted_iota(jnp.int32, (tq, tk), 1)
        return jnp.concatenate([c <= r, c <= r], axis=0)

    def finish(acc):
        lp = lam_ref[...]
        lam = (jnp.exp(jnp.sum(lp[0:1] * lp[1:2], axis=-1, keepdims=True))
               - jnp.exp(jnp.sum(lp[2:3] * lp[3:4], axis=-1, keepdims=True)) + lam_init)
        o = acc[:, 0:A_V_DIM] / acc[:, A_V_DIM:A_V_DIM + LANES]
        o = o[0:tq] - lam * o[tq:rows]
        o = (o * lax.rsqrt(jnp.mean(o * o, axis=-1, keepdims=True) + RMS_EPS)) * sw_ref[...]
        o = o * (1.0 - lam_init)
        o_ref[0] = (o * _silu(g_ref[0])).astype(o_ref.dtype)

    _causal_flash(t, tk, scores, next_scores0, meta_scores, diag_mask, finish,
                  v_ref, vm_ref, sa_ref, sb_ref, sc_ref, m_ref, acc_ref)


def _pos_columns(pos):
    a = (pos // POS_RADIX).astype(jnp.float32)[:, None]
    b = (pos % POS_RADIX).astype(jnp.float32)[:, None]
    ln = jnp.arange(LANES)[None, :]
    return jnp.where(ln < 3, a, jnp.where(ln < 6, b, 0.0)).astype(jnp.bfloat16)


def _diff_attn(slopes, qkv, kv_meta, gates, lam_p, subln_w, tq, tk, lam_init):
    b, s, _ = qkv.shape
    hq = A_HEADS
    pos_x = _pos_columns(jnp.arange(s, dtype=jnp.int32) + N_META)
    pos_m = _pos_columns(jnp.arange(META_PAD, dtype=jnp.int32))
    assert tq == tk
    last = s // tq - 1
    return pl.pallas_call(
        functools.partial(_diff_attn_kernel, tq=tq, tk=tk, lam_init=lam_init),
        grid=(b, hq, s // tq),
        in_specs=[
            pl.BlockSpec(memory_space=pltpu.SMEM),
            pl.BlockSpec((1, tq, LANES), lambda b, h, t: (b, t, h)),
            pl.BlockSpec((1, tq, LANES), lambda b, h, t: (b, jnp.minimum(t + 1, last), h)),
            pl.BlockSpec((1, s, LANES), lambda b, h, t: (b, 0, hq + h)),
            pl.BlockSpec((1, s, LANES), lambda b, h, t: (b, 0, 2 * hq + h)),
            pl.BlockSpec((s, LANES), lambda b, h, t: (0, 0)),
            pl.BlockSpec((META_PAD, LANES), lambda b, h, t: (0, hq + h)),
            pl.BlockSpec((META_PAD, LANES), lambda b, h, t: (0, 2 * hq + h)),
            pl.BlockSpec((META_PAD, LANES), lambda b, h, t: (0, 0)),
            pl.BlockSpec((1, tq, LANES), lambda b, h, t: (b, t, h)),
            pl.BlockSpec(lam_p.shape, lambda b, h, t: (0, 0)),
            pl.BlockSpec(subln_w.shape, lambda b, h, t: (0, 0)),
        ],
        out_specs=pl.BlockSpec((1, tq, LANES), lambda b, h, t: (b, t, h)),
        out_shape=jax.ShapeDtypeStruct((b, s, A_WIDTH), jnp.bfloat16),
        scratch_shapes=[pltpu.VMEM((2 * tq, 2 * LANES), jnp.bfloat16),
                        pltpu.VMEM((2 * tq, tk), jnp.float32),
                        pltpu.VMEM((2 * tq, tk), jnp.float32),
                        pltpu.VMEM((2 * tq, tk), jnp.float32),
                        pltpu.VMEM((2 * tq, LANES), jnp.float32),
                        pltpu.VMEM((2 * tq, A_V_DIM + LANES), jnp.float32)],
        compiler_params=_cparams(3),
        name="diff_attn",
    )(slopes, qkv, qkv, qkv, qkv, pos_x, kv_meta, kv_meta, pos_m, gates, lam_p, subln_w)


def _mla_attn_kernel(q_ref, qn_ref, k_ref, v_ref, km_ref, vm_ref, g_ref, o_ref,
                     sa_ref, sb_ref, sc_ref, m_ref, acc_ref, *, tq, tk):
    t = pl.program_id(2)

    def scores(kt):
        k0 = pl.multiple_of(kt * tk, tk)
        return lax.dot_general(q_ref[0], k_ref[0, pl.ds(k0, tk), :], _NT,
                               preferred_element_type=jnp.float32)

    def next_scores0():
        return lax.dot_general(qn_ref[0], k_ref[0, 0:tk, :], _NT,
                               preferred_element_type=jnp.float32)

    def meta_scores():
        s = lax.dot_general(q_ref[0], km_ref[...], _NT, preferred_element_type=jnp.float32)
        cm = lax.broadcasted_iota(jnp.int32, (tq, META_PAD), 1)
        return jnp.where(cm < N_META, s, -jnp.inf)

    def diag_mask():
        r = lax.broadcasted_iota(jnp.int32, (tq, tk), 0)
        c = lax.broadcasted_iota(jnp.int32, (tq, tk), 1)
        return c <= r

    def finish(acc):
        o = acc[:, 0:B_V_DIM] / acc[:, B_V_DIM:B_V_DIM + LANES]
        o_ref[0] = (o * _silu(g_ref[0])).astype(o_ref.dtype)

    _causal_flash(t, tk, scores, next_scores0, meta_scores, diag_mask, finish,
                  v_ref, vm_ref, sa_ref, sb_ref, sc_ref, m_ref, acc_ref)


def _mla_attn(q_cat, k_cat, v, km, vm, gates, tq, tk):
    b, s, _ = q_cat.shape
    g_blk0 = A_WIDTH // LANES
    assert tq == tk
    last = s // tq - 1
    return pl.pallas_call(
        functools.partial(_mla_attn_kernel, tq=tq, tk=tk),
        grid=(b, B_HEADS, s // tq),
        in_specs=[
            pl.BlockSpec((1, tq, B_QK_PAD), lambda b, h, t: (b, t, h)),
            pl.BlockSpec((1, tq, B_QK_PAD), lambda b, h, t: (b, jnp.minimum(t + 1, last), h)),
            pl.BlockSpec((1, s, B_QK_PAD), lambda b, h, t: (b, 0, h)),
            pl.BlockSpec((1, s, B_V_DIM), lambda b, h, t: (b, 0, h)),
            pl.BlockSpec((META_PAD, B_QK_PAD), lambda b, h, t: (0, h)),
            pl.BlockSpec((META_PAD, B_V_DIM), lambda b, h, t: (0, h)),
            pl.BlockSpec((1, tq, LANES), lambda b, h, t: (b, t, g_blk0 + h)),
        ],
        out_specs=pl.BlockSpec((1, tq, B_V_DIM), lambda b, h, t: (b, t, h)),
        out_shape=jax.ShapeDtypeStruct((b, s, B_WIDTH), jnp.bfloat16),
        scratch_shapes=[pltpu.VMEM((tq, tk), jnp.float32),
                        pltpu.VMEM((tq, tk), jnp.float32),
                        pltpu.VMEM((tq, tk), jnp.float32),
                        pltpu.VMEM((tq, LANES), jnp.float32),
                        pltpu.VMEM((tq, B_V_DIM + LANES), jnp.float32)],
        compiler_params=_cparams(3),
        name="mla_attn",
    )(q_cat, q_cat, k_cat, v, km, vm, gates)


def _out_proj_kernel(oa_ref, ob_ref, x_ref, wa_ref, wb_ref, fw_ref, y_ref):
    d = jnp.dot(oa_ref[...], wa_ref[...], preferred_element_type=jnp.float32)
    d = d + jnp.dot(ob_ref[...], wb_ref[...], preferred_element_type=jnp.float32)
    hres = x_ref[...] + d
    r = lax.rsqrt(jnp.mean(hres * hres, axis=-1, keepdims=True) + RMS_EPS)
    y_ref[...] = (hres * r) * fw_ref[...]


def _out_proj(oa, ob, x, wa, wb, fw, tm):
    rows, d = x.shape
    full = lambda a: pl.BlockSpec(a.shape, lambda i: (0, 0))
    return pl.pallas_call(
        _out_proj_kernel,
        grid=(rows // tm,),
        in_specs=[pl.BlockSpec((tm, oa.shape[1]), lambda i: (i, 0)),
                  pl.BlockSpec((tm, ob.shape[1]), lambda i: (i, 0)),
                  pl.BlockSpec((tm, d), lambda i: (i, 0)),
                  full(wa), full(wb), full(fw)],
        out_specs=pl.BlockSpec((tm, d), lambda i: (i, 0)),
        out_shape=jax.ShapeDtypeStruct((rows, d), jnp.float32),
        compiler_params=_cparams(1),
        name="out_proj",
    )(oa, ob, x, wa, wb, fw)


def _rot_half_cols(w):
    half = w.shape[-1] // 2
    return jnp.concatenate([-w[..., half:], w[..., :half]], axis=-1)


def _pad_cols(w, width):
    return jnp.pad(w, ((0, 0), (0, width - w.shape[1])))


def kernel(x, meta_tokens, attn_norm_w, w_in, diff_lambda, diff_subln_w, mla_q_norm_w, w_uq,
           mla_kv_norm_w, w_ukv, w_out, final_norm_w):
    bsz, seq, d = x.shape
    bf16 = jnp.bfloat16
    l = 0
    lam_init = 0.8 - 0.6 * math.exp(-0.3 * l)

    wi = w_in[l]
    a_q, a_k, a_v, a_g, b_cq, b_ckv, b_kr, b_g = jnp.split(
        wi, [1024, 2048, 3072, 4096, 4608, 4864, 4928], axis=1)
    w_all = jnp.concatenate(
        [a_q * (A_QK_DIM ** -0.5 * LOG2E), a_k, a_v,
         a_g, b_g,
         b_cq, b_ckv, _pad_cols(b_kr, LANES), _pad_cols(_rot_half_cols(b_kr), LANES)],
        axis=1).astype(bf16)
    widths = (3 * A_WIDTH, A_WIDTH + B_WIDTH, B_Q_LORA + B_KV_LORA + 2 * LANES)

    wuq = w_uq[l].reshape(B_Q_LORA, B_HEADS, B_NOPE + B_ROPE)
    wq_main = jnp.pad(wuq, ((0, 0), (0, 0), (0, B_QK_PAD - B_NOPE - B_ROPE)))
    wq_main = wq_main.reshape(B_Q_LORA, B_HEADS * B_QK_PAD).astype(bf16)
    wq_rot = jnp.pad(_rot_half_cols(wuq[..., B_NOPE:]), ((0, 0), (0, 0), (0, LANES - B_ROPE)))
    wq_rot = wq_rot.reshape(B_Q_LORA, B_HEADS * LANES).astype(bf16)
    wukv = w_ukv[l].reshape(B_KV_LORA, B_HEADS, B_NOPE + B_V_DIM)
    wk = wukv[..., :B_NOPE].reshape(B_KV_LORA, B_HEADS * B_NOPE).astype(bf16)
    wv = wukv[..., B_NOPE:].reshape(B_KV_LORA, B_WIDTH).astype(bf16)
    wo = w_out[l].astype(bf16)

    half = B_ROPE // 2
    inv = ROPE_THETA ** (-jnp.arange(half, dtype=jnp.float32) / half)
    inv = jnp.tile(inv, LANES // half)[None, :]
    slopes = jnp.exp2(-8.0 * jnp.arange(1, A_HEADS + 1, dtype=jnp.float32) / A_HEADS)

    nw = attn_norm_w[l][None, :]
    qnw = mla_q_norm_w[l][None, :]
    kvnw = mla_kv_norm_w[l][None, :]

    x2 = x.reshape(bsz * seq, d)
    qkv, gates, small = _in_proj(x2, nw, w_all, widths, 512)
    meta = meta_tokens.astype(x.dtype)
    qkv_m, _, small_m = _in_proj(meta, nw, w_all, widths, N_META)

    q_cat, k_cat, v_b = _mla_proj(small, qnw, kvnw, inv, wq_main, wq_rot, wk, wv, 512, seq, N_META)
    _, k_cat_m, v_b_m = _mla_proj(small_m, qnw, kvnw, inv, wq_main, wq_rot, wk, wv, N_META,
                                  N_META, 0)

    pad_meta = lambda a: jnp.pad(a, ((0, META_PAD - N_META), (0, 0)))

    o_a = _diff_attn(slopes, qkv.reshape(bsz, seq, -1), pad_meta(qkv_m),
                     gates.reshape(bsz, seq, -1), diff_lambda[l].astype(jnp.float32),
                     diff_subln_w[l][None, :], 512, 512, lam_init)
    o_b = _mla_attn(q_cat.reshape(bsz, seq, -1), k_cat.reshape(bsz, seq, -1),
                    v_b.reshape(bsz, seq, -1), pad_meta(k_cat_m), pad_meta(v_b_m),
                    gates.reshape(bsz, seq, -1), 1024, 1024)

    y = _out_proj(o_a.reshape(bsz * seq, -1), o_b.reshape(bsz * seq, -1), x2,
                  wo[:A_WIDTH], wo[A_WIDTH:], final_norm_w[None, :], 512)
    return y.reshape(bsz, seq, d)
```

```python
import functools
import math

import jax
import jax.numpy as jnp
from jax import lax
from jax.experimental import pallas as pl
from jax.experimental.pallas import tpu as pltpu

N_META = 16
RMS_EPS = 1e-6
ROPE_THETA = 10000.0
LOG2E = math.log2(math.e)

A_HEADS = 8
A_QK_DIM = 64
A_V_DIM = 128
A_WIDTH = A_HEADS * A_V_DIM
B_HEADS = 8
B_Q_LORA = 512
B_KV_LORA = 256
B_NOPE = 128
B_ROPE = 64
B_V_DIM = 128
B_WIDTH = B_HEADS * B_V_DIM
B_QK_PAD = 256

LANES = 128
META_PAD = 128
POS_RADIX = 256

SMALL_KR = B_Q_LORA + B_KV_LORA
SMALL_KR_ROT = SMALL_KR + LANES
SMALL_WIDTH = SMALL_KR_ROT + LANES

IN_PROJ_TM = 512
IN_PROJ_TN = 512
MLA_PROJ_TM = 512
OUT_PROJ_TM = 512
DIFF_ATTN_T = 512
MLA_ATTN_T = 1024
VMEM_LIMIT = 48 * 1024 * 1024
IN_PROJ_VMEM_LIMIT = 56 * 1024 * 1024

_NT = (((1,), (1,)), ((), ()))


def _cparams(n_grid, vmem_limit=VMEM_LIMIT):
    return pltpu.CompilerParams(
        dimension_semantics=("arbitrary",) * n_grid, vmem_limit_bytes=vmem_limit)


def _in_proj_kernel(x_ref, nw_ref, w_ref, qkv_ref, gates_ref, small_ref, xn_ref, *, tn):
    x = x_ref[...]
    r = lax.rsqrt(jnp.mean(x * x, axis=-1, keepdims=True) + RMS_EPS)
    xn_ref[...] = ((x * r) * nw_ref[...]).astype(xn_ref.dtype)
    j = 0
    for out_ref in (qkv_ref, gates_ref, small_ref):
        for c in range(0, out_ref.shape[1], tn):
            out_ref[:, c:c + tn] = jnp.dot(
                xn_ref[...], w_ref[j], preferred_element_type=jnp.float32).astype(out_ref.dtype)
            j += 1


def _in_proj(x, nw, w_all, widths, tm):
    rows, k = x.shape
    tn = w_all.shape[2]
    assert sum(widths) == w_all.shape[0] * tn and all(w % tn == 0 for w in widths)
    return pl.pallas_call(
        functools.partial(_in_proj_kernel, tn=tn),
        grid=(rows // tm,),
        in_specs=[
            pl.BlockSpec((tm, k), lambda i: (i, 0)),
            pl.BlockSpec((1, k), lambda i: (0, 0)),
            pl.BlockSpec(w_all.shape, lambda i: (0, 0, 0), pipeline_mode=pl.Buffered(1)),
        ],
        out_specs=[pl.BlockSpec((tm, w), lambda i: (i, 0)) for w in widths],
        out_shape=[jax.ShapeDtypeStruct((rows, widths[0]), jnp.bfloat16),
                   jax.ShapeDtypeStruct((rows, widths[1]), jnp.float32),
                   jax.ShapeDtypeStruct((rows, widths[2]), jnp.float32)],
        scratch_shapes=[pltpu.VMEM((tm, k), jnp.bfloat16)],
        compiler_params=_cparams(1, IN_PROJ_VMEM_LIMIT),
        name="in_proj",
    )(x, nw, w_all)


def _mla_proj_kernel(small_ref, qnw_ref, kvnw_ref, inv_ref, wq_ref, wqr_ref, wk_ref, wv_ref,
                     q_ref, k_ref, v_ref, *, tm, seq, pos0, scale):
    def rms(x, w):
        r = lax.rsqrt(jnp.mean(x * x, axis=-1, keepdims=True) + RMS_EPS)
        return (x * r) * w

    cq = rms(small_ref[:, 0:B_Q_LORA], qnw_ref[...]).astype(jnp.bfloat16)
    ckv = rms(small_ref[:, B_Q_LORA:B_Q_LORA + B_KV_LORA], kvnw_ref[...]).astype(jnp.bfloat16)
    kr = small_ref[:, SMALL_KR:SMALL_KR + LANES]
    kr_rot = small_ref[:, SMALL_KR_ROT:SMALL_KR_ROT + LANES]

    row = lax.broadcasted_iota(jnp.int32, (tm, LANES), 0)
    pos = (row + ((pl.program_id(0) * tm) % seq + pos0)).astype(jnp.float32)
    ang = pos * inv_ref[...]
    cos, sin = jnp.cos(ang), jnp.sin(ang)

    q1 = jnp.dot(cq, wq_ref[...], preferred_element_type=jnp.float32)
    q2 = jnp.dot(cq, wqr_ref[...], preferred_element_type=jnp.float32)
    kn = jnp.dot(ckv, wk_ref[...], preferred_element_type=jnp.float32)
    v_ref[...] = jnp.dot(ckv, wv_ref[...], preferred_element_type=jnp.float32).astype(v_ref.dtype)
    kpe = (kr * cos + kr_rot * sin).astype(k_ref.dtype)
    for h in range(B_HEADS):
        c0 = h * B_QK_PAD
        q_ref[:, c0:c0 + LANES] = (q1[:, c0:c0 + LANES] * scale).astype(q_ref.dtype)
        qpe = q1[:, c0 + LANES:c0 + 2 * LANES] * cos + q2[:, h * LANES:(h + 1) * LANES] * sin
        q_ref[:, c0 + LANES:c0 + 2 * LANES] = (qpe * scale).astype(q_ref.dtype)
        k_ref[:, c0:c0 + LANES] = kn[:, h * LANES:(h + 1) * LANES].astype(k_ref.dtype)
        k_ref[:, c0 + LANES:c0 + 2 * LANES] = kpe


def _mla_proj(small, qnw, kvnw, inv, wq, wqr, wk, wv, tm, seq, pos0):
    rows = small.shape[0]
    full = lambda a: pl.BlockSpec(a.shape, lambda i: (0, 0))
    scale = (B_NOPE + B_ROPE) ** -0.5 * LOG2E
    return pl.pallas_call(
        functools.partial(_mla_proj_kernel, tm=tm, seq=seq, pos0=pos0, scale=scale),
        grid=(rows // tm,),
        in_specs=[pl.BlockSpec((tm, small.shape[1]), lambda i: (i, 0)),
                  full(qnw), full(kvnw), full(inv), full(wq), full(wqr), full(wk), full(wv)],
        out_specs=[pl.BlockSpec((tm, B_HEADS * B_QK_PAD), lambda i: (i, 0)),
                   pl.BlockSpec((tm, B_HEADS * B_QK_PAD), lambda i: (i, 0)),
                   pl.BlockSpec((tm, B_WIDTH), lambda i: (i, 0))],
        out_shape=[jax.ShapeDtypeStruct((rows, B_HEADS * B_QK_PAD), jnp.bfloat16),
                   jax.ShapeDtypeStruct((rows, B_HEADS * B_QK_PAD), jnp.bfloat16),
                   jax.ShapeDtypeStruct((rows, B_WIDTH), jnp.bfloat16)],
        compiler_params=_cparams(1),
        name="mla_proj",
    )(small, qnw, kvnw, inv, wq, wqr, wk, wv)


def _silu(g):
    return g * (1.0 / (1.0 + jnp.exp(-g)))


def _lanes(x, width):
    return jnp.tile(x, (1, width // LANES))


def _with_ones(v):
    return jnp.concatenate([v, jnp.ones((v.shape[0], LANES), v.dtype)], axis=1)


def _online_step(s, v, m_ref, acc_ref, finish=None):
    tk = s.shape[-1]
    m_prev = m_ref[...]
    m_new = jnp.maximum(m_prev, jnp.max(s, axis=-1, keepdims=True))
    alpha = jnp.exp2(m_prev - m_new)
    p = jnp.exp2(s - _lanes(m_new, tk))
    pv = jnp.dot(p.astype(v.dtype), _with_ones(v), preferred_element_type=jnp.float32)
    acc = _lanes(alpha, acc_ref.shape[-1]) * acc_ref[...] + pv
    if finish is None:
        acc_ref[...] = acc
        m_ref[...] = m_new
    else:
        finish(acc)


def _first_step(s, v, m_ref, acc_ref):
    tk = s.shape[-1]
    m_new = jnp.broadcast_to(jnp.max(s, axis=-1, keepdims=True), m_ref.shape)
    p = jnp.exp2(s - _lanes(m_new, tk))
    acc_ref[...] = jnp.dot(p.astype(v.dtype), _with_ones(v), preferred_element_type=jnp.float32)
    m_ref[...] = m_new


def _causal_flash(t, tk, scores, next_scores0, meta_scores, diag_mask, finish,
                  v_ref, vm_ref, sa_ref, sb_ref, sc_ref, m_ref, acc_ref):
    def softmax_pv(s_ref, kt, masked):
        k0 = pl.multiple_of(kt * tk, tk)
        s = s_ref[...]
        if masked:
            s = jnp.where(diag_mask(), s, -jnp.inf)
        _online_step(s, v_ref[0, pl.ds(k0, tk), :], m_ref, acc_ref,
                     finish if masked else None)

    @pl.when(t == 0)
    def _():
        sc_ref[...] = scores(0)
        _first_step(meta_scores(), vm_ref[...], m_ref, acc_ref)
        softmax_pv(sc_ref, 0, True)
        sc_ref[...] = next_scores0()

    @pl.when(t > 0)
    def _():
        sb_ref[...] = scores(1)
        _first_step(sc_ref[...], v_ref[0, 0:tk, :], m_ref, acc_ref)
        _online_step(meta_scores(), vm_ref[...], m_ref, acc_ref)

        def pair(j):
            sa_ref[...] = scores(2 * j + 2)
            softmax_pv(sb_ref, 2 * j + 1, False)
            sb_ref[...] = scores(2 * j + 3)
            softmax_pv(sa_ref, 2 * j + 2, False)

        def two_pairs(j, carry):
            pair(2 * j)
            pair(2 * j + 1)
            return carry

        n_pairs = (t - 1) // 2
        lax.fori_loop(0, n_pairs // 2, two_pairs, 0)

        @pl.when(n_pairs % 2 == 1)
        def _():
            pair(n_pairs - 1)

        kt = 2 * n_pairs + 1

        @pl.when(kt == t)
        def _():
            sc_ref[...] = next_scores0()
            softmax_pv(sb_ref, kt, True)

        @pl.when(kt < t)
        def _():
            sa_ref[...] = scores(kt + 1)
            softmax_pv(sb_ref, kt, False)
            sc_ref[...] = next_scores0()
            softmax_pv(sa_ref, kt + 1, True)


def _diff_attn_kernel(slopes_ref, q_ref, qn_ref, k_ref, v_ref, pk_ref, km_ref, vm_ref, pkm_ref,
                      g_ref, lam_ref, sw_ref, o_ref, qs_ref, sa_ref, sb_ref, sc_ref, m_ref,
                      acc_ref, *, tq, tk, lam_init):
    h = pl.program_id(1)
    t = pl.program_id(2)
    rows = 2 * tq

    @pl.when(t == 0)
    def _():
        slope = jnp.full((1, LANES), slopes_ref[h] * LOG2E, jnp.float32)
        s0 = slope.astype(jnp.bfloat16).astype(jnp.float32)
        s1 = (slope - s0).astype(jnp.bfloat16).astype(jnp.float32)
        s2 = (slope - s0 - s1).astype(jnp.bfloat16).astype(jnp.float32)
        ln = lax.broadcasted_iota(jnp.int32, (1, LANES), 1)
        piece = jnp.where(ln % 3 == 0, s0, jnp.where(ln % 3 == 1, s1, s2))
        q_pos = jnp.where(ln < 3, piece * float(POS_RADIX), jnp.where(ln < 6, piece, 0.0))
        qs_ref[:, LANES:2 * LANES] = jnp.broadcast_to(q_pos, (rows, LANES)).astype(qs_ref.dtype)

    def stacked(qb):
        lane = lax.broadcasted_iota(jnp.int32, (tq, LANES), 1)
        zero = jnp.zeros_like(qb)
        return jnp.concatenate([jnp.where(lane < A_QK_DIM, qb, zero),
                                jnp.where(lane >= A_QK_DIM, qb, zero)], axis=0)

    def tile_scores(qs, k0):
        kk = jnp.concatenate([k_ref[0, pl.ds(k0, tk), :], pk_ref[pl.ds(k0, tk), :]], axis=1)
        return lax.dot_general(qs, kk, _NT, preferred_element_type=jnp.float32)

    qs_ref[:, 0:LANES] = stacked(q_ref[0])

    def scores(kt):
        return tile_scores(qs_ref[...], pl.multiple_of(kt * tk, tk))

    def next_scores0():
        qn = jnp.concatenate([stacked(qn_ref[0]), qs_ref[:, LANES:2 * LANES]], axis=1)
        return tile_scores(qn, 0)

    def meta_scores():
        kk = jnp.concatenate([km_ref[...], pkm_ref[...]], axis=1)
        s = lax.dot_general(qs_ref[...], kk, _NT, preferred_element_type=jnp.float32)
        cm = lax.broadcasted_iota(jnp.int32, (rows, META_PAD), 1)
        return jnp.where(cm < N_META, s, -jnp.inf)

    def diag_mask():
        r = lax.broadcasted_iota(jnp.int32, (tq, tk), 0)
        c = lax.broadcasted_iota(jnp.int32, (tq, tk), 1)
        return jnp.concatenate([c <= r, c <= r], axis=0)

    def finish(acc):
        lp = lam_ref[...]
        lam = (jnp.exp(jnp.sum(lp[0:1] * lp[1:2], axis=-1, keepdims=True))
               - jnp.exp(jnp.sum(lp[2:3] * lp[3:4], axis=-1, keepdims=True)) + lam_init)
        o = acc[:, 0:A_V_DIM] / acc[:, A_V_DIM:A_V_DIM + LANES]
        o = o[0:tq] - lam * o[tq:rows]
        o = (o * lax.rsqrt(jnp.mean(o * o, axis=-1, keepdims=True) + RMS_EPS)) * sw_ref[...]
        o = o * (1.0 - lam_init)
        o_ref[0] = (o * _silu(g_ref[0])).astype(o_ref.dtype)

    _causal_flash(t, tk, scores, next_scores0, meta_scores, diag_mask, finish,
                  v_ref, vm_ref, sa_ref, sb_ref, sc_ref, m_ref, acc_ref)


def _pos_columns(pos):
    a = (pos // POS_RADIX).astype(jnp.float32)[:, None]
    b = (pos % POS_RADIX).astype(jnp.float32)[:, None]
    ln = jnp.arange(LANES)[None, :]
    return jnp.where(ln < 3, a, jnp.where(ln < 6, b, 0.0)).astype(jnp.bfloat16)


def _diff_attn(slopes, qkv, kv_meta, gates, lam_p, subln_w, tq, tk, lam_init):
    b, s, _ = qkv.shape
    hq = A_HEADS
    pos_x = _pos_columns(jnp.arange(s, dtype=jnp.int32) + N_META)
    pos_m = _pos_columns(jnp.arange(META_PAD, dtype=jnp.int32))
    assert tq == tk
    last = s // tq - 1
    return pl.pallas_call(
        functools.partial(_diff_attn_kernel, tq=tq, tk=tk, lam_init=lam_init),
        grid=(b, hq, s // tq),
        in_specs=[
            pl.BlockSpec(memory_space=pltpu.SMEM),
            pl.BlockSpec((1, tq, LANES), lambda b, h, t: (b, t, h)),
            pl.BlockSpec((1, tq, LANES), lambda b, h, t: (b, jnp.minimum(t + 1, last), h)),
            pl.BlockSpec((1, s, LANES), lambda b, h, t: (b, 0, hq + h)),
            pl.BlockSpec((1, s, LANES), lambda b, h, t: (b, 0, 2 * hq + h)),
            pl.BlockSpec((s, LANES), lambda b, h, t: (0, 0)),
            pl.BlockSpec((META_PAD, LANES), lambda b, h, t: (0, hq + h)),
            pl.BlockSpec((META_PAD, LANES), lambda b, h, t: (0, 2 * hq + h)),
            pl.BlockSpec((META_PAD, LANES), lambda b, h, t: (0, 0)),
            pl.BlockSpec((1, tq, LANES), lambda b, h, t: (b, t, h)),
            pl.BlockSpec(lam_p.shape, lambda b, h, t: (0, 0)),
            pl.BlockSpec(subln_w.shape, lambda b, h, t: (0, 0)),
        ],
        out_specs=pl.BlockSpec((1, tq, LANES), lambda b, h, t: (b, t, h)),
        out_shape=jax.ShapeDtypeStruct((b, s, A_WIDTH), jnp.bfloat16),
        scratch_shapes=[pltpu.VMEM((2 * tq, 2 * LANES), jnp.bfloat16),
                        pltpu.VMEM((2 * tq, tk), jnp.float32),
                        pltpu.VMEM((2 * tq, tk), jnp.float32),
                        pltpu.VMEM((2 * tq, tk), jnp.float32),
                        pltpu.VMEM((2 * tq, LANES), jnp.float32),
                        pltpu.VMEM((2 * tq, A_V_DIM + LANES), jnp.float32)],
        compiler_params=_cparams(3),
        name="diff_attn",
    )(slopes, qkv, qkv, qkv, qkv, pos_x, kv_meta, kv_meta, pos_m, gates, lam_p, subln_w)


def _mla_attn_kernel(q_ref, qn_ref, k_ref, v_ref, km_ref, vm_ref, g_ref, o_ref,
                     sa_ref, sb_ref, sc_ref, m_ref, acc_ref, *, tq, tk):
    t = pl.program_id(2)

    def scores(kt):
        k0 = pl.multiple_of(kt * tk, tk)
        return lax.dot_general(q_ref[0], k_ref[0, pl.ds(k0, tk), :], _NT,
                               preferred_element_type=jnp.float32)

    def next_scores0():
        return lax.dot_general(qn_ref[0], k_ref[0, 0:tk, :], _NT,
                               preferred_element_type=jnp.float32)

    def meta_scores():
        s = lax.dot_general(q_ref[0], km_ref[...], _NT, preferred_element_type=jnp.float32)
        cm = lax.broadcasted_iota(jnp.int32, (tq, META_PAD), 1)
        return jnp.where(cm < N_META, s, -jnp.inf)

    def diag_mask():
        r = lax.broadcasted_iota(jnp.int32, (tq, tk), 0)
        c = lax.broadcasted_iota(jnp.int32, (tq, tk), 1)
        return c <= r

    def finish(acc):
        o = acc[:, 0:B_V_DIM] / acc[:, B_V_DIM:B_V_DIM + LANES]
        o_ref[0] = (o * _silu(g_ref[0])).astype(o_ref.dtype)

    _causal_flash(t, tk, scores, next_scores0, meta_scores, diag_mask, finish,
                  v_ref, vm_ref, sa_ref, sb_ref, sc_ref, m_ref, acc_ref)


def _mla_attn(q_cat, k_cat, v, km, vm, gates, tq, tk):
    b, s, _ = q_cat.shape
    g_blk0 = A_WIDTH // LANES
    assert tq == tk
    last = s // tq - 1
    return pl.pallas_call(
        functools.partial(_mla_attn_kernel, tq=tq, tk=tk),
        grid=(b, B_HEADS, s // tq),
        in_specs=[
            pl.BlockSpec((1, tq, B_QK_PAD), lambda b, h, t: (b, t, h)),
            pl.BlockSpec((1, tq, B_QK_PAD), lambda b, h, t: (b, jnp.minimum(t + 1, last), h)),
            pl.BlockSpec((1, s, B_QK_PAD), lambda b, h, t: (b, 0, h)),
            pl.BlockSpec((1, s, B_V_DIM), lambda b, h, t: (b, 0, h)),
            pl.BlockSpec((META_PAD, B_QK_PAD), lambda b, h, t: (0, h)),
            pl.BlockSpec((META_PAD, B_V_DIM), lambda b, h, t: (0, h)),
            pl.BlockSpec((1, tq, LANES), lambda b, h, t: (b, t, g_blk0 + h)),
        ],
        out_specs=pl.BlockSpec((1, tq, B_V_DIM), lambda b, h, t: (b, t, h)),
        out_shape=jax.ShapeDtypeStruct((b, s, B_WIDTH), jnp.bfloat16),
        scratch_shapes=[pltpu.VMEM((tq, tk), jnp.float32),
                        pltpu.VMEM((tq, tk), jnp.float32),
                        pltpu.VMEM((tq, tk), jnp.float32),
                        pltpu.VMEM((tq, LANES), jnp.float32),
                        pltpu.VMEM((tq, B_V_DIM + LANES), jnp.float32)],
        compiler_params=_cparams(3),
        name="mla_attn",
    )(q_cat, q_cat, k_cat, v, km, vm, gates)


def _out_proj_kernel(oa_ref, ob_ref, x_ref, wa_ref, wb_ref, fw_ref, y_ref):
    d = jnp.dot(oa_ref[...], wa_ref[...], preferred_element_type=jnp.float32)
    d = d + jnp.dot(ob_ref[...], wb_ref[...], preferred_element_type=jnp.float32)
    hres = x_ref[...] + d
    r = lax.rsqrt(jnp.mean(hres * hres, axis=-1, keepdims=True) + RMS_EPS)
    y_ref[...] = (hres * r) * fw_ref[...]


def _out_proj(oa, ob, x, wa, wb, fw, tm):
    rows, d = x.shape
    full = lambda a: pl.BlockSpec(a.shape, lambda i: (0, 0))
    return pl.pallas_call(
        _out_proj_kernel,
        grid=(rows // tm,),
        in_specs=[pl.BlockSpec((tm, oa.shape[1]), lambda i: (i, 0)),
                  pl.BlockSpec((tm, ob.shape[1]), lambda i: (i, 0)),
                  pl.BlockSpec((tm, d), lambda i: (i, 0)),
                  full(wa), full(wb), full(fw)],
        out_specs=pl.BlockSpec((tm, d), lambda i: (i, 0)),
        out_shape=jax.ShapeDtypeStruct((rows, d), jnp.float32),
        compiler_params=_cparams(1),
        name="out_proj",
    )(oa, ob, x, wa, wb, fw)


def _rot_half_cols(w):
    half = w.shape[-1] // 2
    return jnp.concatenate([-w[..., half:], w[..., :half]], axis=-1)


def _pad_cols(w, width):
    return jnp.pad(w, ((0, 0), (0, width - w.shape[1])))


def kernel(x, meta_tokens, attn_norm_w, w_in, diff_lambda, diff_subln_w, mla_q_norm_w, w_uq,
           mla_kv_norm_w, w_ukv, w_out, final_norm_w):
    bsz, seq, d = x.shape
    bf16 = jnp.bfloat16
    l = 0
    lam_init = 0.8 - 0.6 * math.exp(-0.3 * l)

    wi = w_in[l]
    in_cols = (A_HEADS * 2 * A_QK_DIM, A_HEADS * 2 * A_QK_DIM, A_WIDTH, A_WIDTH,
               B_Q_LORA, B_KV_LORA, B_ROPE)
    a_q, a_k, a_v, a_g, b_cq, b_ckv, b_kr, b_g = jnp.split(
        wi, [sum(in_cols[:n + 1]) for n in range(len(in_cols))], axis=1)
    w_all = jnp.concatenate(
        [a_q * (A_QK_DIM ** -0.5 * LOG2E), a_k, a_v,
         a_g, b_g,
         b_cq, b_ckv, _pad_cols(b_kr, LANES), _pad_cols(_rot_half_cols(b_kr), LANES)],
        axis=1).astype(bf16)
    widths = (3 * A_WIDTH, A_WIDTH + B_WIDTH, SMALL_WIDTH)

    wuq = w_uq[l].reshape(B_Q_LORA, B_HEADS, B_NOPE + B_ROPE)
    wq_main = jnp.pad(wuq, ((0, 0), (0, 0), (0, B_QK_PAD - B_NOPE - B_ROPE)))
    wq_main = wq_main.reshape(B_Q_LORA, B_HEADS * B_QK_PAD).astype(bf16)
    wq_rot = jnp.pad(_rot_half_cols(wuq[..., B_NOPE:]), ((0, 0), (0, 0), (0, LANES - B_ROPE)))
    wq_rot = wq_rot.reshape(B_Q_LORA, B_HEADS * LANES).astype(bf16)
    wukv = w_ukv[l].reshape(B_KV_LORA, B_HEADS, B_NOPE + B_V_DIM)
    wk = wukv[..., :B_NOPE].reshape(B_KV_LORA, B_HEADS * B_NOPE).astype(bf16)
    wv = wukv[..., B_NOPE:].reshape(B_KV_LORA, B_WIDTH).astype(bf16)
    wo = w_out[l].astype(bf16)

    half = B_ROPE // 2
    inv = ROPE_THETA ** (-jnp.arange(half, dtype=jnp.float32) / half)
    inv = jnp.tile(inv, LANES // half)[None, :]
    slopes = jnp.exp2(-8.0 * jnp.arange(1, A_HEADS + 1, dtype=jnp.float32) / A_HEADS)

    nw = attn_norm_w[l][None, :]
    qnw = mla_q_norm_w[l][None, :]
    kvnw = mla_kv_norm_w[l][None, :]

    x2 = x.reshape(bsz * seq, d)
    w_all = w_all.reshape(d, -1, IN_PROJ_TN).transpose(1, 0, 2)
    qkv, gates, small = _in_proj(x2, nw, w_all, widths, IN_PROJ_TM)
    meta = meta_tokens.astype(x.dtype)
    qkv_m, _, small_m = _in_proj(meta, nw, w_all, widths, N_META)

    q_cat, k_cat, v_b = _mla_proj(small, qnw, kvnw, inv, wq_main, wq_rot, wk, wv, MLA_PROJ_TM,
                                  seq, N_META)
    _, k_cat_m, v_b_m = _mla_proj(small_m, qnw, kvnw, inv, wq_main, wq_rot, wk, wv, N_META,
                                  N_META, 0)

    pad_meta = lambda a: jnp.pad(a, ((0, META_PAD - N_META), (0, 0)))

    o_a = _diff_attn(slopes, qkv.reshape(bsz, seq, -1), pad_meta(qkv_m),
                     gates.reshape(bsz, seq, -1), diff_lambda[l].astype(jnp.float32),
                     diff_subln_w[l][None, :], DIFF_ATTN_T, DIFF_ATTN_T, lam_init)
    o_b = _mla_attn(q_cat.reshape(bsz, seq, -1), k_cat.reshape(bsz, seq, -1),
                    v_b.reshape(bsz, seq, -1), pad_meta(k_cat_m), pad_meta(v_b_m),
                    gates.reshape(bsz, seq, -1), MLA_ATTN_T, MLA_ATTN_T)

    y = _out_proj(o_a.reshape(bsz * seq, -1), o_b.reshape(bsz * seq, -1), x2,
                  wo[:A_WIDTH], wo[A_WIDTH:], final_norm_w[None, :], OUT_PROJ_TM)
    return y.reshape(bsz, seq, d)
```

```python
import functools
import math

import jax
import jax.numpy as jnp
from jax import lax
from jax.experimental import pallas as pl
from jax.experimental.pallas import tpu as pltpu

N_META = 16
RMS_EPS = 1e-6
ROPE_THETA = 10000.0
LOG2E = math.log2(math.e)

A_HEADS = 8
A_QK_DIM = 64
A_V_DIM = 128
A_WIDTH = A_HEADS * A_V_DIM
B_HEADS = 8
B_Q_LORA = 512
B_KV_LORA = 256
B_NOPE = 128
B_ROPE = 64
B_V_DIM = 128
B_WIDTH = B_HEADS * B_V_DIM
B_QK_PAD = 256

LANES = 128
META_PAD = 128
POS_RADIX = 256

SMALL_KR = B_Q_LORA + B_KV_LORA
SMALL_KR_ROT = SMALL_KR + LANES
SMALL_WIDTH = SMALL_KR_ROT + LANES

IN_PROJ_TM = 512
IN_PROJ_TN = 512
MLA_PROJ_TM = 512
OUT_PROJ_TM = 512
DIFF_ATTN_T = 512
MLA_ATTN_T = 1024
VMEM_LIMIT = 48 * 1024 * 1024
IN_PROJ_VMEM_LIMIT = 56 * 1024 * 1024
MLA_ATTN_VMEM_LIMIT = 56 * 1024 * 1024

_NT = (((1,), (1,)), ((), ()))


def _cparams(n_grid, vmem_limit=VMEM_LIMIT):
    return pltpu.CompilerParams(
        dimension_semantics=("arbitrary",) * n_grid, vmem_limit_bytes=vmem_limit)


def _in_proj_kernel(x_ref, nw_ref, w_ref, qkv_ref, gates_ref, small_ref, xn_ref, *, tn):
    x = x_ref[...]
    r = lax.rsqrt(jnp.mean(x * x, axis=-1, keepdims=True) + RMS_EPS)
    xn_ref[...] = ((x * r) * nw_ref[...]).astype(xn_ref.dtype)
    j = 0
    for out_ref in (qkv_ref, gates_ref, small_ref):
        for c in range(0, out_ref.shape[1], tn):
            out_ref[:, c:c + tn] = jnp.dot(
                xn_ref[...], w_ref[j], preferred_element_type=jnp.float32).astype(out_ref.dtype)
            j += 1


def _in_proj(x, nw, w_all, widths, tm):
    rows, k = x.shape
    tn = w_all.shape[2]
    assert sum(widths) == w_all.shape[0] * tn and all(w % tn == 0 for w in widths)
    return pl.pallas_call(
        functools.partial(_in_proj_kernel, tn=tn),
        grid=(rows // tm,),
        in_specs=[
            pl.BlockSpec((tm, k), lambda i: (i, 0)),
            pl.BlockSpec((1, k), lambda i: (0, 0)),
            pl.BlockSpec(w_all.shape, lambda i: (0, 0, 0), pipeline_mode=pl.Buffered(1)),
        ],
        out_specs=[pl.BlockSpec((tm, w), lambda i: (i, 0)) for w in widths],
        out_shape=[jax.ShapeDtypeStruct((rows, widths[0]), jnp.bfloat16),
                   jax.ShapeDtypeStruct((rows, widths[1]), jnp.float32),
                   jax.ShapeDtypeStruct((rows, widths[2]), jnp.float32)],
        scratch_shapes=[pltpu.VMEM((tm, k), jnp.bfloat16)],
        compiler_params=_cparams(1, IN_PROJ_VMEM_LIMIT),
        name="in_proj",
    )(x, nw, w_all)


def _mla_proj_kernel(small_ref, qnw_ref, kvnw_ref, inv_ref, wq_ref, wqr_ref, wk_ref, wv_ref,
                     q_ref, k_ref, v_ref, *, tm, seq, pos0, scale):
    def rms(x, w):
        r = lax.rsqrt(jnp.mean(x * x, axis=-1, keepdims=True) + RMS_EPS)
        return (x * r) * w

    cq = rms(small_ref[:, 0:B_Q_LORA], qnw_ref[...]).astype(jnp.bfloat16)
    ckv = rms(small_ref[:, B_Q_LORA:B_Q_LORA + B_KV_LORA], kvnw_ref[...]).astype(jnp.bfloat16)
    kr = small_ref[:, SMALL_KR:SMALL_KR + LANES]
    kr_rot = small_ref[:, SMALL_KR_ROT:SMALL_KR_ROT + LANES]

    row = lax.broadcasted_iota(jnp.int32, (tm, LANES), 0)
    pos = (row + ((pl.program_id(0) * tm) % seq + pos0)).astype(jnp.float32)
    ang = pos * inv_ref[...]
    cos, sin = jnp.cos(ang), jnp.sin(ang)

    q1 = jnp.dot(cq, wq_ref[...], preferred_element_type=jnp.float32)
    q2 = jnp.dot(cq, wqr_ref[...], preferred_element_type=jnp.float32)
    kn = jnp.dot(ckv, wk_ref[...], preferred_element_type=jnp.float32)
    v_ref[...] = jnp.dot(ckv, wv_ref[...], preferred_element_type=jnp.float32).astype(v_ref.dtype)
    kpe = (kr * cos + kr_rot * sin).astype(k_ref.dtype)
    for h in range(B_HEADS):
        c0 = h * B_QK_PAD
        q_ref[:, c0:c0 + LANES] = (q1[:, c0:c0 + LANES] * scale).astype(q_ref.dtype)
        qpe = q1[:, c0 + LANES:c0 + 2 * LANES] * cos + q2[:, h * LANES:(h + 1) * LANES] * sin
        q_ref[:, c0 + LANES:c0 + 2 * LANES] = (qpe * scale).astype(q_ref.dtype)
        k_ref[:, c0:c0 + LANES] = kn[:, h * LANES:(h + 1) * LANES].astype(k_ref.dtype)
        k_ref[:, c0 + LANES:c0 + 2 * LANES] = kpe


def _mla_proj(small, qnw, kvnw, inv, wq, wqr, wk, wv, tm, seq, pos0):
    rows = small.shape[0]
    full = lambda a: pl.BlockSpec(a.shape, lambda i: (0, 0))
    scale = (B_NOPE + B_ROPE) ** -0.5 * LOG2E
    return pl.pallas_call(
        functools.partial(_mla_proj_kernel, tm=tm, seq=seq, pos0=pos0, scale=scale),
        grid=(rows // tm,),
        in_specs=[pl.BlockSpec((tm, small.shape[1]), lambda i: (i, 0)),
                  full(qnw), full(kvnw), full(inv), full(wq), full(wqr), full(wk), full(wv)],
        out_specs=[pl.BlockSpec((tm, B_HEADS * B_QK_PAD), lambda i: (i, 0)),
                   pl.BlockSpec((tm, B_HEADS * B_QK_PAD), lambda i: (i, 0)),
                   pl.BlockSpec((tm, B_WIDTH), lambda i: (i, 0))],
        out_shape=[jax.ShapeDtypeStruct((rows, B_HEADS * B_QK_PAD), jnp.bfloat16),
                   jax.ShapeDtypeStruct((rows, B_HEADS * B_QK_PAD), jnp.bfloat16),
                   jax.ShapeDtypeStruct((rows, B_WIDTH), jnp.bfloat16)],
        compiler_params=_cparams(1),
        name="mla_proj",
    )(small, qnw, kvnw, inv, wq, wqr, wk, wv)


def _silu(g):
    return g * (1.0 / (1.0 + jnp.exp(-g)))


def _lanes(x, width):
    return jnp.tile(x, (1, width // LANES))


def _with_ones(v):
    return jnp.concatenate([v, jnp.ones((v.shape[0], LANES), v.dtype)], axis=1)


def _online_step(s, v, m_ref, acc_ref, finish=None):
    tk = s.shape[-1]
    m_prev = m_ref[...]
    m_new = jnp.maximum(m_prev, jnp.max(s, axis=-1, keepdims=True))
    alpha = jnp.exp2(m_prev - m_new)
    p = jnp.exp2(s - _lanes(m_new, tk))
    pv = jnp.dot(p.astype(v.dtype), _with_ones(v), preferred_element_type=jnp.float32)
    acc = _lanes(alpha, acc_ref.shape[-1]) * acc_ref[...] + pv
    if finish is None:
        acc_ref[...] = acc
        m_ref[...] = m_new
    else:
        finish(acc)


def _first_step(s, v, m_ref, acc_ref):
    tk = s.shape[-1]
    m_new = jnp.broadcast_to(jnp.max(s, axis=-1, keepdims=True), m_ref.shape)
    p = jnp.exp2(s - _lanes(m_new, tk))
    acc_ref[...] = jnp.dot(p.astype(v.dtype), _with_ones(v), preferred_element_type=jnp.float32)
    m_ref[...] = m_new


def _causal_flash(t, first, tk, scores, next_scores0, meta_scores, diag_mask, finish,
                  v_ref, vm_ref, sa_ref, sb_ref, sc_ref, m_ref, acc_ref):
    def softmax_pv(s_ref, kt, masked):
        k0 = pl.multiple_of(kt * tk, tk)
        s = s_ref[...]
        if masked:
            s = jnp.where(diag_mask(), s, -jnp.inf)
        _online_step(s, v_ref[0, pl.ds(k0, tk), :], m_ref, acc_ref,
                     finish if masked else None)

    if first:
        sc_ref[...] = scores(0)
        _first_step(meta_scores(), vm_ref[...], m_ref, acc_ref)
        softmax_pv(sc_ref, 0, True)
        sc_ref[...] = next_scores0()
        return

    sb_ref[...] = scores(1)
    _first_step(sc_ref[...], v_ref[0, 0:tk, :], m_ref, acc_ref)
    _online_step(meta_scores(), vm_ref[...], m_ref, acc_ref)

    def pair(j):
        sa_ref[...] = scores(2 * j + 2)
        softmax_pv(sb_ref, 2 * j + 1, False)
        sb_ref[...] = scores(2 * j + 3)
        softmax_pv(sa_ref, 2 * j + 2, False)

    def two_pairs(j, carry):
        pair(2 * j)
        pair(2 * j + 1)
        return carry

    n_pairs = (t - 1) // 2
    lax.fori_loop(0, n_pairs // 2, two_pairs, 0)

    @pl.when(n_pairs % 2 == 1)
    def _():
        pair(n_pairs - 1)

    kt = 2 * n_pairs + 1

    @pl.when(kt == t)
    def _():
        sc_ref[...] = next_scores0()
        softmax_pv(sb_ref, kt, True)

    @pl.when(kt < t)
    def _():
        sa_ref[...] = scores(kt + 1)
        softmax_pv(sb_ref, kt, False)
        sc_ref[...] = next_scores0()
        softmax_pv(sa_ref, kt + 1, True)


def _diff_attn_kernel(slopes_ref, q_ref, k_ref, v_ref, pk_ref, km_ref, vm_ref, pkm_ref,
                      g_ref, lam_ref, sw_ref, o_ref, qs_ref, sa_ref, sb_ref, sc_ref, m_ref,
                      acc_ref, *, tq, tk, n_t, lam_init):
    h = pl.program_id(1)
    rows = 2 * tq

    slope = jnp.full((1, LANES), slopes_ref[h] * LOG2E, jnp.float32)
    s0 = slope.astype(jnp.bfloat16).astype(jnp.float32)
    s1 = (slope - s0).astype(jnp.bfloat16).astype(jnp.float32)
    s2 = (slope - s0 - s1).astype(jnp.bfloat16).astype(jnp.float32)
    ln = lax.broadcasted_iota(jnp.int32, (1, LANES), 1)
    piece = jnp.where(ln % 3 == 0, s0, jnp.where(ln % 3 == 1, s1, s2))
    q_pos = jnp.where(ln < 3, piece * float(POS_RADIX), jnp.where(ln < 6, piece, 0.0))
    qs_ref[:, LANES:2 * LANES] = jnp.broadcast_to(q_pos, (rows, LANES)).astype(qs_ref.dtype)

    def stacked(t):
        qb = q_ref[0, pl.ds(pl.multiple_of(t * tq, tq), tq), :]
        lane = lax.broadcasted_iota(jnp.int32, (tq, LANES), 1)
        zero = jnp.zeros_like(qb)
        return jnp.concatenate([jnp.where(lane < A_QK_DIM, qb, zero),
                                jnp.where(lane >= A_QK_DIM, qb, zero)], axis=0)

    def tile_scores(qs, k0):
        kk = jnp.concatenate([k_ref[0, pl.ds(k0, tk), :], pk_ref[pl.ds(k0, tk), :]], axis=1)
        return lax.dot_general(qs, kk, _NT, preferred_element_type=jnp.float32)

    def scores(kt):
        return tile_scores(qs_ref[...], pl.multiple_of(kt * tk, tk))

    def meta_scores():
        kk = jnp.concatenate([km_ref[...], pkm_ref[...]], axis=1)
        s = lax.dot_general(qs_ref[...], kk, _NT, preferred_element_type=jnp.float32)
        cm = lax.broadcasted_iota(jnp.int32, (rows, META_PAD), 1)
        return jnp.where(cm < N_META, s, -jnp.inf)

    def diag_mask():
        r = lax.broadcasted_iota(jnp.int32, (tq, tk), 0)
        c = lax.broadcasted_iota(jnp.int32, (tq, tk), 1)
        return jnp.concatenate([c <= r, c <= r], axis=0)

    def q_tile(t, first):
        row0 = pl.multiple_of(t * tq, tq)
        qs_ref[:, 0:LANES] = stacked(t)

        def next_scores0():
            qn = stacked(jnp.minimum(t + 1, n_t - 1))
            return tile_scores(jnp.concatenate([qn, qs_ref[:, LANES:2 * LANES]], axis=1), 0)

        def finish(acc):
            lp = lam_ref[...]
            lam = (jnp.exp(jnp.sum(lp[0:1] * lp[1:2], axis=-1, keepdims=True))
                   - jnp.exp(jnp.sum(lp[2:3] * lp[3:4], axis=-1, keepdims=True)) + lam_init)
            o = acc[:, 0:A_V_DIM] / acc[:, A_V_DIM:A_V_DIM + LANES]
            o = o[0:tq] - lam * o[tq:rows]
            o = (o * lax.rsqrt(jnp.mean(o * o, axis=-1, keepdims=True) + RMS_EPS)) * sw_ref[...]
            o = o * (1.0 - lam_init)
            g = g_ref[0, pl.ds(row0, tq), :]
            o_ref[0, pl.ds(row0, tq), :] = (o * _silu(g)).astype(o_ref.dtype)

        _causal_flash(t, first, tk, scores, next_scores0, meta_scores, diag_mask, finish,
                      v_ref, vm_ref, sa_ref, sb_ref, sc_ref, m_ref, acc_ref)

    q_tile(0, True)

    def body(t, carry):
        q_tile(t, False)
        return carry

    lax.fori_loop(1, n_t, body, 0)


def _pos_columns(pos):
    a = (pos // POS_RADIX).astype(jnp.float32)[:, None]
    b = (pos % POS_RADIX).astype(jnp.float32)[:, None]
    ln = jnp.arange(LANES)[None, :]
    return jnp.where(ln < 3, a, jnp.where(ln < 6, b, 0.0)).astype(jnp.bfloat16)


def _diff_attn(slopes, qkv, kv_meta, gates, lam_p, subln_w, tq, tk, lam_init):
    b, s, _ = qkv.shape
    hq = A_HEADS
    pos_x = _pos_columns(jnp.arange(s, dtype=jnp.int32) + N_META)
    pos_m = _pos_columns(jnp.arange(META_PAD, dtype=jnp.int32))
    assert tq == tk and s % tq == 0
    return pl.pallas_call(
        functools.partial(_diff_attn_kernel, tq=tq, tk=tk, n_t=s // tq, lam_init=lam_init),
        grid=(b, hq),
        in_specs=[
            pl.BlockSpec(memory_space=pltpu.SMEM),
            pl.BlockSpec((1, s, LANES), lambda b, h: (b, 0, h)),
            pl.BlockSpec((1, s, LANES), lambda b, h: (b, 0, hq + h)),
            pl.BlockSpec((1, s, LANES), lambda b, h: (b, 0, 2 * hq + h)),
            pl.BlockSpec((s, LANES), lambda b, h: (0, 0)),
            pl.BlockSpec((META_PAD, LANES), lambda b, h: (0, hq + h)),
            pl.BlockSpec((META_PAD, LANES), lambda b, h: (0, 2 * hq + h)),
            pl.BlockSpec((META_PAD, LANES), lambda b, h: (0, 0)),
            pl.BlockSpec((1, s, LANES), lambda b, h: (b, 0, h)),
            pl.BlockSpec(lam_p.shape, lambda b, h: (0, 0)),
            pl.BlockSpec(subln_w.shape, lambda b, h: (0, 0)),
        ],
        out_specs=pl.BlockSpec((1, s, LANES), lambda b, h: (b, 0, h)),
        out_shape=jax.ShapeDtypeStruct((b, s, A_WIDTH), jnp.bfloat16),
        scratch_shapes=[pltpu.VMEM((2 * tq, 2 * LANES), jnp.bfloat16),
                        pltpu.VMEM((2 * tq, tk), jnp.float32),
                        pltpu.VMEM((2 * tq, tk), jnp.float32),
                        pltpu.VMEM((2 * tq, tk), jnp.float32),
                        pltpu.VMEM((2 * tq, LANES), jnp.float32),
                        pltpu.VMEM((2 * tq, A_V_DIM + LANES), jnp.float32)],
        compiler_params=_cparams(2),
        name="diff_attn",
    )(slopes, qkv, qkv, qkv, pos_x, kv_meta, kv_meta, pos_m, gates, lam_p, subln_w)


def _mla_attn_kernel(q_ref, k_ref, v_ref, km_ref, vm_ref, g_ref, o_ref,
                     sa_ref, sb_ref, sc_ref, m_ref, acc_ref, *, tq, tk, n_t):
    def q_rows(t):
        return q_ref[0, pl.ds(pl.multiple_of(t * tq, tq), tq), :]

    def diag_mask():
        r = lax.broadcasted_iota(jnp.int32, (tq, tk), 0)
        c = lax.broadcasted_iota(jnp.int32, (tq, tk), 1)
        return c <= r

    def q_tile(t, first):
        row0 = pl.multiple_of(t * tq, tq)

        def scores(kt):
            k0 = pl.multiple_of(kt * tk, tk)
            return lax.dot_general(q_rows(t), k_ref[0, pl.ds(k0, tk), :], _NT,
                                   preferred_element_type=jnp.float32)

        def next_scores0():
            return lax.dot_general(q_rows(jnp.minimum(t + 1, n_t - 1)), k_ref[0, 0:tk, :], _NT,
                                   preferred_element_type=jnp.float32)

        def meta_scores():
            s = lax.dot_general(q_rows(t), km_ref[...], _NT, preferred_element_type=jnp.float32)
            cm = lax.broadcasted_iota(jnp.int32, (tq, META_PAD), 1)
            return jnp.where(cm < N_META, s, -jnp.inf)

        def finish(acc):
            o = acc[:, 0:B_V_DIM] / acc[:, B_V_DIM:B_V_DIM + LANES]
            g = g_ref[0, pl.ds(row0, tq), :]
            o_ref[0, pl.ds(row0, tq), :] = (o * _silu(g)).astype(o_ref.dtype)

        _causal_flash(t, first, tk, scores, next_scores0, meta_scores, diag_mask, finish,
                      v_ref, vm_ref, sa_ref, sb_ref, sc_ref, m_ref, acc_ref)

    q_tile(0, True)

    def body(t, carry):
        q_tile(t, False)
        return carry

    lax.fori_loop(1, n_t, body, 0)


def _mla_attn(q_cat, k_cat, v, km, vm, gates, tq, tk):
    b, s, _ = q_cat.shape
    g_blk0 = A_WIDTH // LANES
    assert tq == tk and s % tq == 0
    return pl.pallas_call(
        functools.partial(_mla_attn_kernel, tq=tq, tk=tk, n_t=s // tq),
        grid=(b, B_HEADS),
        in_specs=[
            pl.BlockSpec((1, s, B_QK_PAD), lambda b, h: (b, 0, h)),
            pl.BlockSpec((1, s, B_QK_PAD), lambda b, h: (b, 0, h)),
            pl.BlockSpec((1, s, B_V_DIM), lambda b, h: (b, 0, h)),
            pl.BlockSpec((META_PAD, B_QK_PAD), lambda b, h: (0, h)),
            pl.BlockSpec((META_PAD, B_V_DIM), lambda b, h: (0, h)),
            pl.BlockSpec((1, s, LANES), lambda b, h: (b, 0, g_blk0 + h)),
        ],
        out_specs=pl.BlockSpec((1, s, B_V_DIM), lambda b, h: (b, 0, h)),
        out_shape=jax.ShapeDtypeStruct((b, s, B_WIDTH), jnp.bfloat16),
        scratch_shapes=[pltpu.VMEM((tq, tk), jnp.float32),
                        pltpu.VMEM((tq, tk), jnp.float32),
                        pltpu.VMEM((tq, tk), jnp.float32),
                        pltpu.VMEM((tq, LANES), jnp.float32),
                        pltpu.VMEM((tq, B_V_DIM + LANES), jnp.float32)],
        compiler_params=_cparams(2, MLA_ATTN_VMEM_LIMIT),
        name="mla_attn",
    )(q_cat, k_cat, v, km, vm, gates)


def _out_proj_kernel(oa_ref, ob_ref, x_ref, wa_ref, wb_ref, fw_ref, y_ref):
    d = jnp.dot(oa_ref[...], wa_ref[...], preferred_element_type=jnp.float32)
    d = d + jnp.dot(ob_ref[...], wb_ref[...], preferred_element_type=jnp.float32)
    hres = x_ref[...] + d
    r = lax.rsqrt(jnp.mean(hres * hres, axis=-1, keepdims=True) + RMS_EPS)
    y_ref[...] = (hres * r) * fw_ref[...]


def _out_proj(oa, ob, x, wa, wb, fw, tm):
    rows, d = x.shape
    full = lambda a: pl.BlockSpec(a.shape, lambda i: (0, 0))
    return pl.pallas_call(
        _out_proj_kernel,
        grid=(rows // tm,),
        in_specs=[pl.BlockSpec((tm, oa.shape[1]), lambda i: (i, 0)),
                  pl.BlockSpec((tm, ob.shape[1]), lambda i: (i, 0)),
                  pl.BlockSpec((tm, d), lambda i: (i, 0)),
                  full(wa), full(wb), full(fw)],
        out_specs=pl.BlockSpec((tm, d), lambda i: (i, 0)),
        out_shape=jax.ShapeDtypeStruct((rows, d), jnp.float32),
        compiler_params=_cparams(1),
        name="out_proj",
    )(oa, ob, x, wa, wb, fw)


def _rot_half_cols(w):
    half = w.shape[-1] // 2
    return jnp.concatenate([-w[..., half:], w[..., :half]], axis=-1)


def _pad_cols(w, width):
    return jnp.pad(w, ((0, 0), (0, width - w.shape[1])))


def kernel(x, meta_tokens, attn_norm_w, w_in, diff_lambda, diff_subln_w, mla_q_norm_w, w_uq,
           mla_kv_norm_w, w_ukv, w_out, final_norm_w):
    bsz, seq, d = x.shape
    bf16 = jnp.bfloat16
    l = 0
    lam_init = 0.8 - 0.6 * math.exp(-0.3 * l)

    wi = w_in[l]
    in_cols = (A_HEADS * 2 * A_QK_DIM, A_HEADS * 2 * A_QK_DIM, A_WIDTH, A_WIDTH,
               B_Q_LORA, B_KV_LORA, B_ROPE)
    a_q, a_k, a_v, a_g, b_cq, b_ckv, b_kr, b_g = jnp.split(
        wi, [sum(in_cols[:n + 1]) for n in range(len(in_cols))], axis=1)
    w_all = jnp.concatenate(
        [a_q * (A_QK_DIM ** -0.5 * LOG2E), a_k, a_v,
         a_g, b_g,
         b_cq, b_ckv, _pad_cols(b_kr, LANES), _pad_cols(_rot_half_cols(b_kr), LANES)],
        axis=1).astype(bf16)
    widths = (3 * A_WIDTH, A_WIDTH + B_WIDTH, SMALL_WIDTH)

    wuq = w_uq[l].reshape(B_Q_LORA, B_HEADS, B_NOPE + B_ROPE)
    wq_main = jnp.pad(wuq, ((0, 0), (0, 0), (0, B_QK_PAD - B_NOPE - B_ROPE)))
    wq_main = wq_main.reshape(B_Q_LORA, B_HEADS * B_QK_PAD).astype(bf16)
    wq_rot = jnp.pad(_rot_half_cols(wuq[..., B_NOPE:]), ((0, 0), (0, 0), (0, LANES - B_ROPE)))
    wq_rot = wq_rot.reshape(B_Q_LORA, B_HEADS * LANES).astype(bf16)
    wukv = w_ukv[l].reshape(B_KV_LORA, B_HEADS, B_NOPE + B_V_DIM)
    wk = wukv[..., :B_NOPE].reshape(B_KV_LORA, B_HEADS * B_NOPE).astype(bf16)
    wv = wukv[..., B_NOPE:].reshape(B_KV_LORA, B_WIDTH).astype(bf16)
    wo = w_out[l].astype(bf16)

    half = B_ROPE // 2
    inv = ROPE_THETA ** (-jnp.arange(half, dtype=jnp.float32) / half)
    inv = jnp.tile(inv, LANES // half)[None, :]
    slopes = jnp.exp2(-8.0 * jnp.arange(1, A_HEADS + 1, dtype=jnp.float32) / A_HEADS)

    nw = attn_norm_w[l][None, :]
    qnw = mla_q_norm_w[l][None, :]
    kvnw = mla_kv_norm_w[l][None, :]

    x2 = x.reshape(bsz * seq, d)
    w_all = w_all.reshape(d, -1, IN_PROJ_TN).transpose(1, 0, 2)
    qkv, gates, small = _in_proj(x2, nw, w_all, widths, IN_PROJ_TM)
    meta = meta_tokens.astype(x.dtype)
    qkv_m, _, small_m = _in_proj(meta, nw, w_all, widths, N_META)

    q_cat, k_cat, v_b = _mla_proj(small, qnw, kvnw, inv, wq_main, wq_rot, wk, wv, MLA_PROJ_TM,
                                  seq, N_META)
    _, k_cat_m, v_b_m = _mla_proj(small_m, qnw, kvnw, inv, wq_main, wq_rot, wk, wv, N_META,
                                  N_META, 0)

    pad_meta = lambda a: jnp.pad(a, ((0, META_PAD - N_META), (0, 0)))

    o_a = _diff_attn(slopes, qkv.reshape(bsz, seq, -1), pad_meta(qkv_m),
                     gates.reshape(bsz, seq, -1), diff_lambda[l].astype(jnp.float32),
                     diff_subln_w[l][None, :], DIFF_ATTN_T, DIFF_ATTN_T, lam_init)
    o_b = _mla_attn(q_cat.reshape(bsz, seq, -1), k_cat.reshape(bsz, seq, -1),
                    v_b.reshape(bsz, seq, -1), pad_meta(k_cat_m), pad_meta(v_b_m),
                    gates.reshape(bsz, seq, -1), MLA_ATTN_T, MLA_ATTN_T)

    y = _out_proj(o_a.reshape(bsz * seq, -1), o_b.reshape(bsz * seq, -1), x2,
                  wo[:A_WIDTH], wo[A_WIDTH:], final_norm_w[None, :], OUT_PROJ_TM)
    return y.reshape(bsz, seq, d)
```

```python
import functools
import math

import jax
import jax.numpy as jnp
from jax import lax
from jax.experimental import pallas as pl
from jax.experimental.pallas import tpu as pltpu

N_META = 16
RMS_EPS = 1e-6
ROPE_THETA = 10000.0
LOG2E = math.log2(math.e)

A_HEADS = 8
A_QK_DIM = 64
A_V_DIM = 128
A_WIDTH = A_HEADS * A_V_DIM
B_HEADS = 8
B_Q_LORA = 512
B_KV_LORA = 256
B_NOPE = 128
B_ROPE = 64
B_V_DIM = 128
B_WIDTH = B_HEADS * B_V_DIM
B_QK_PAD = 256

LANES = 128
META_PAD = 128
POS_RADIX = 256

SMALL_KR = B_Q_LORA + B_KV_LORA
SMALL_KR_ROT = SMALL_KR + LANES
SMALL_WIDTH = SMALL_KR_ROT + LANES

IN_PROJ_TM = 512
IN_PROJ_TN = 512
MLA_PROJ_TM = 1024
OUT_PROJ_TM = 512
DIFF_ATTN_T = 512
MLA_ATTN_T = 1024
VMEM_LIMIT = 48 * 1024 * 1024
IN_PROJ_VMEM_LIMIT = 56 * 1024 * 1024
MLA_ATTN_VMEM_LIMIT = 56 * 1024 * 1024

_NT = (((1,), (1,)), ((), ()))


def _cparams(n_grid, vmem_limit=VMEM_LIMIT):
    return pltpu.CompilerParams(
        dimension_semantics=("arbitrary",) * n_grid, vmem_limit_bytes=vmem_limit)


def _in_proj_kernel(x_ref, nw_ref, w_ref, qkv_ref, gates_ref, small_ref, xn_ref, *, tn):
    x = x_ref[...]
    r = lax.rsqrt(jnp.mean(x * x, axis=-1, keepdims=True) + RMS_EPS)
    xn_ref[...] = ((x * r) * nw_ref[...]).astype(xn_ref.dtype)
    j = 0
    for out_ref in (qkv_ref, gates_ref, small_ref):
        for c in range(0, out_ref.shape[1], tn):
            out_ref[:, c:c + tn] = jnp.dot(
                xn_ref[...], w_ref[j], preferred_element_type=jnp.float32).astype(out_ref.dtype)
            j += 1


def _in_proj(x, nw, w_all, widths, tm):
    rows, k = x.shape
    tn = w_all.shape[2]
    assert sum(widths) == w_all.shape[0] * tn and all(w % tn == 0 for w in widths)
    return pl.pallas_call(
        functools.partial(_in_proj_kernel, tn=tn),
        grid=(rows // tm,),
        in_specs=[
            pl.BlockSpec((tm, k), lambda i: (i, 0)),
            pl.BlockSpec((1, k), lambda i: (0, 0)),
            pl.BlockSpec(w_all.shape, lambda i: (0, 0, 0), pipeline_mode=pl.Buffered(1)),
        ],
        out_specs=[pl.BlockSpec((tm, w), lambda i: (i, 0)) for w in widths],
        out_shape=[jax.ShapeDtypeStruct((rows, widths[0]), jnp.bfloat16),
                   jax.ShapeDtypeStruct((rows, widths[1]), jnp.float32),
                   jax.ShapeDtypeStruct((rows, widths[2]), jnp.float32)],
        scratch_shapes=[pltpu.VMEM((tm, k), jnp.bfloat16)],
        compiler_params=_cparams(1, IN_PROJ_VMEM_LIMIT),
        name="in_proj",
    )(x, nw, w_all)


def _mla_proj_kernel(small_ref, qnw_ref, kvnw_ref, inv_ref, wq_ref, wqr_ref, wk_ref, wv_ref,
                     q_ref, k_ref, v_ref, *, tm, seq, pos0, scale):
    def rms(x, w):
        r = lax.rsqrt(jnp.mean(x * x, axis=-1, keepdims=True) + RMS_EPS)
        return (x * r) * w

    cq = rms(small_ref[:, 0:B_Q_LORA], qnw_ref[...]).astype(jnp.bfloat16)
    ckv = rms(small_ref[:, B_Q_LORA:B_Q_LORA + B_KV_LORA], kvnw_ref[...]).astype(jnp.bfloat16)
    kr = small_ref[:, SMALL_KR:SMALL_KR + LANES]
    kr_rot = small_ref[:, SMALL_KR_ROT:SMALL_KR_ROT + LANES]

    row = lax.broadcasted_iota(jnp.int32, (tm, LANES), 0)
    pos = (row + ((pl.program_id(0) * tm) % seq + pos0)).astype(jnp.float32)
    ang = pos * inv_ref[...]
    cos, sin = jnp.cos(ang), jnp.sin(ang)

    q1 = jnp.dot(cq, wq_ref[...], preferred_element_type=jnp.float32)
    q2 = jnp.dot(cq, wqr_ref[...], preferred_element_type=jnp.float32)
    kn = jnp.dot(ckv, wk_ref[...], preferred_element_type=jnp.float32)
    v_ref[...] = jnp.dot(ckv, wv_ref[...], preferred_element_type=jnp.float32).astype(v_ref.dtype)
    kpe = (kr * cos + kr_rot * sin).astype(k_ref.dtype)
    for h in range(B_HEADS):
        c0 = h * B_QK_PAD
        q_ref[:, c0:c0 + LANES] = (q1[:, c0:c0 + LANES] * scale).astype(q_ref.dtype)
        qpe = q1[:, c0 + LANES:c0 + 2 * LANES] * cos + q2[:, h * LANES:(h + 1) * LANES] * sin
        q_ref[:, c0 + LANES:c0 + 2 * LANES] = (qpe * scale).astype(q_ref.dtype)
        k_ref[:, c0:c0 + LANES] = kn[:, h * LANES:(h + 1) * LANES].astype(k_ref.dtype)
        k_ref[:, c0 + LANES:c0 + 2 * LANES] = kpe


def _mla_proj(small, qnw, kvnw, inv, wq, wqr, wk, wv, tm, seq, pos0):
    rows = small.shape[0]
    full = lambda a: pl.BlockSpec(a.shape, lambda i: (0, 0))
    scale = (B_NOPE + B_ROPE) ** -0.5 * LOG2E
    return pl.pallas_call(
        functools.partial(_mla_proj_kernel, tm=tm, seq=seq, pos0=pos0, scale=scale),
        grid=(rows // tm,),
        in_specs=[pl.BlockSpec((tm, small.shape[1]), lambda i: (i, 0)),
                  full(qnw), full(kvnw), full(inv), full(wq), full(wqr), full(wk), full(wv)],
        out_specs=[pl.BlockSpec((tm, B_HEADS * B_QK_PAD), lambda i: (i, 0)),
                   pl.BlockSpec((tm, B_HEADS * B_QK_PAD), lambda i: (i, 0)),
                   pl.BlockSpec((tm, B_WIDTH), lambda i: (i, 0))],
        out_shape=[jax.ShapeDtypeStruct((rows, B_HEADS * B_QK_PAD), jnp.bfloat16),
                   jax.ShapeDtypeStruct((rows, B_HEADS * B_QK_PAD), jnp.bfloat16),
                   jax.ShapeDtypeStruct((rows, B_WIDTH), jnp.bfloat16)],
        compiler_params=_cparams(1),
        name="mla_proj",
    )(small, qnw, kvnw, inv, wq, wqr, wk, wv)


def _silu(g):
    return g * (1.0 / (1.0 + jnp.exp(-g)))


def _lanes(x, width):
    return jnp.tile(x, (1, width // LANES))


def _with_ones(v):
    return jnp.concatenate([v, jnp.ones((v.shape[0], LANES), v.dtype)], axis=1)


def _online_step(s, v, m_ref, acc_ref, finish=None):
    tk = s.shape[-1]
    m_prev = m_ref[...]
    m_new = jnp.maximum(m_prev, jnp.max(s, axis=-1, keepdims=True))
    alpha = jnp.exp2(m_prev - m_new)
    p = jnp.exp2(s - _lanes(m_new, tk))
    pv = jnp.dot(p.astype(v.dtype), _with_ones(v), preferred_element_type=jnp.float32)
    acc = _lanes(alpha, acc_ref.shape[-1]) * acc_ref[...] + pv
    if finish is None:
        acc_ref[...] = acc
        m_ref[...] = m_new
    else:
        finish(acc)


def _first_step(s, v, m_ref, acc_ref):
    tk = s.shape[-1]
    m_new = jnp.broadcast_to(jnp.max(s, axis=-1, keepdims=True), m_ref.shape)
    p = jnp.exp2(s - _lanes(m_new, tk))
    acc_ref[...] = jnp.dot(p.astype(v.dtype), _with_ones(v), preferred_element_type=jnp.float32)
    m_ref[...] = m_new


def _causal_flash(t, kind, tk, scores, next_scores0, meta_scores, diag_mask, finish,
                  v_ref, vm_ref, sa_ref, sb_ref, sc_ref, m_ref, acc_ref):
    def softmax_pv(s_ref, kt, masked):
        k0 = pl.multiple_of(kt * tk, tk)
        s = s_ref[...]
        if masked:
            s = jnp.where(diag_mask(), s, -jnp.inf)
        _online_step(s, v_ref[0, pl.ds(k0, tk), :], m_ref, acc_ref,
                     finish if masked else None)

    if kind == "first":
        sc_ref[...] = scores(0)
        _first_step(meta_scores(), vm_ref[...], m_ref, acc_ref)
        softmax_pv(sc_ref, 0, True)
        sc_ref[...] = next_scores0()
        return

    sb_ref[...] = scores(1)
    _first_step(sc_ref[...], v_ref[0, 0:tk, :], m_ref, acc_ref)
    _online_step(meta_scores(), vm_ref[...], m_ref, acc_ref)

    def pair(j):
        sa_ref[...] = scores(2 * j + 2)
        softmax_pv(sb_ref, 2 * j + 1, False)
        sb_ref[...] = scores(2 * j + 3)
        softmax_pv(sa_ref, 2 * j + 2, False)

    def two_pairs(j, carry):
        pair(2 * j)
        pair(2 * j + 1)
        return carry

    n_pairs = (t - 1) // 2
    lax.fori_loop(0, n_pairs // 2, two_pairs, 0)

    @pl.when(n_pairs % 2 == 1)
    def _():
        pair(n_pairs - 1)

    kt = 2 * n_pairs + 1
    if kind == "odd":
        sc_ref[...] = next_scores0()
        softmax_pv(sb_ref, kt, True)
    else:
        sa_ref[...] = scores(kt + 1)
        softmax_pv(sb_ref, kt, False)
        sc_ref[...] = next_scores0()
        softmax_pv(sa_ref, kt + 1, True)


def _walk_q_tiles(n_t, q_tile):
    q_tile(0, "first")

    def two_tiles(u, carry):
        q_tile(2 * u + 1, "odd")
        q_tile(2 * u + 2, "even")
        return carry

    lax.fori_loop(0, (n_t - 1) // 2, two_tiles, 0)
    if n_t % 2 == 0:
        q_tile(jnp.int32(n_t - 1), "odd")


def _diff_attn_kernel(slopes_ref, q_ref, k_ref, v_ref, pk_ref, km_ref, vm_ref, pkm_ref,
                      g_ref, lam_ref, sw_ref, o_ref, qs_ref, sa_ref, sb_ref, sc_ref, m_ref,
                      acc_ref, *, tq, tk, n_t, lam_init):
    h = pl.program_id(1)
    rows = 2 * tq

    slope = jnp.full((1, LANES), slopes_ref[h] * LOG2E, jnp.float32)
    s0 = slope.astype(jnp.bfloat16).astype(jnp.float32)
    s1 = (slope - s0).astype(jnp.bfloat16).astype(jnp.float32)
    s2 = (slope - s0 - s1).astype(jnp.bfloat16).astype(jnp.float32)
    ln = lax.broadcasted_iota(jnp.int32, (1, LANES), 1)
    piece = jnp.where(ln % 3 == 0, s0, jnp.where(ln % 3 == 1, s1, s2))
    q_pos = jnp.where(ln < 3, piece * float(POS_RADIX), jnp.where(ln < 6, piece, 0.0))
    qs_ref[:, LANES:2 * LANES] = jnp.broadcast_to(q_pos, (rows, LANES)).astype(qs_ref.dtype)

    def stacked(t):
        qb = q_ref[0, pl.ds(pl.multiple_of(t * tq, tq), tq), :]
        lane = lax.broadcasted_iota(jnp.int32, (tq, LANES), 1)
        zero = jnp.zeros_like(qb)
        return jnp.concatenate([jnp.where(lane < A_QK_DIM, qb, zero),
                                jnp.where(lane >= A_QK_DIM, qb, zero)], axis=0)

    def tile_scores(qs, k0):
        kk = jnp.concatenate([k_ref[0, pl.ds(k0, tk), :], pk_ref[pl.ds(k0, tk), :]], axis=1)
        return lax.dot_general(qs, kk, _NT, preferred_element_type=jnp.float32)

    def scores(kt):
        return tile_scores(qs_ref[...], pl.multiple_of(kt * tk, tk))

    def meta_scores():
        kk = jnp.concatenate([km_ref[...], pkm_ref[...]], axis=1)
        s = lax.dot_general(qs_ref[...], kk, _NT, preferred_element_type=jnp.float32)
        cm = lax.broadcasted_iota(jnp.int32, (rows, META_PAD), 1)
        return jnp.where(cm < N_META, s, -jnp.inf)

    def diag_mask():
        r = lax.broadcasted_iota(jnp.int32, (tq, tk), 0)
        c = lax.broadcasted_iota(jnp.int32, (tq, tk), 1)
        return jnp.concatenate([c <= r, c <= r], axis=0)

    def q_tile(t, kind):
        row0 = pl.multiple_of(t * tq, tq)
        qs_ref[:, 0:LANES] = stacked(t)

        def next_scores0():
            qn = stacked(jnp.minimum(t + 1, n_t - 1))
            return tile_scores(jnp.concatenate([qn, qs_ref[:, LANES:2 * LANES]], axis=1), 0)

        def finish(acc):
            lp = lam_ref[...]
            lam = (jnp.exp(jnp.sum(lp[0:1] * lp[1:2], axis=-1, keepdims=True))
                   - jnp.exp(jnp.sum(lp[2:3] * lp[3:4], axis=-1, keepdims=True)) + lam_init)
            o = acc[:, 0:A_V_DIM] / acc[:, A_V_DIM:A_V_DIM + LANES]
            o = o[0:tq] - lam * o[tq:rows]
            o = (o * lax.rsqrt(jnp.mean(o * o, axis=-1, keepdims=True) + RMS_EPS)) * sw_ref[...]
            o = o * (1.0 - lam_init)
            g = g_ref[0, pl.ds(row0, tq), :]
            o_ref[0, pl.ds(row0, tq), :] = (o * _silu(g)).astype(o_ref.dtype)

        _causal_flash(t, kind, tk, scores, next_scores0, meta_scores, diag_mask, finish,
                      v_ref, vm_ref, sa_ref, sb_ref, sc_ref, m_ref, acc_ref)

    _walk_q_tiles(n_t, q_tile)


def _pos_columns(pos):
    a = (pos // POS_RADIX).astype(jnp.float32)[:, None]
    b = (pos % POS_RADIX).astype(jnp.float32)[:, None]
    ln = jnp.arange(LANES)[None, :]
    return jnp.where(ln < 3, a, jnp.where(ln < 6, b, 0.0)).astype(jnp.bfloat16)


def _diff_attn(slopes, qkv, kv_meta, gates, lam_p, subln_w, tq, tk, lam_init):
    b, s, _ = qkv.shape
    hq = A_HEADS
    pos_x = _pos_columns(jnp.arange(s, dtype=jnp.int32) + N_META)
    pos_m = _pos_columns(jnp.arange(META_PAD, dtype=jnp.int32))
    assert tq == tk and s % tq == 0
    return pl.pallas_call(
        functools.partial(_diff_attn_kernel, tq=tq, tk=tk, n_t=s // tq, lam_init=lam_init),
        grid=(b, hq),
        in_specs=[
            pl.BlockSpec(memory_space=pltpu.SMEM),
            pl.BlockSpec((1, s, LANES), lambda b, h: (b, 0, h)),
            pl.BlockSpec((1, s, LANES), lambda b, h: (b, 0, hq + h)),
            pl.BlockSpec((1, s, LANES), lambda b, h: (b, 0, 2 * hq + h)),
            pl.BlockSpec((s, LANES), lambda b, h: (0, 0)),
            pl.BlockSpec((META_PAD, LANES), lambda b, h: (0, hq + h)),
            pl.BlockSpec((META_PAD, LANES), lambda b, h: (0, 2 * hq + h)),
            pl.BlockSpec((META_PAD, LANES), lambda b, h: (0, 0)),
            pl.BlockSpec((1, s, LANES), lambda b, h: (b, 0, h)),
            pl.BlockSpec(lam_p.shape, lambda b, h: (0, 0)),
            pl.BlockSpec(subln_w.shape, lambda b, h: (0, 0)),
        ],
        out_specs=pl.BlockSpec((1, s, LANES), lambda b, h: (b, 0, h)),
        out_shape=jax.ShapeDtypeStruct((b, s, A_WIDTH), jnp.bfloat16),
        scratch_shapes=[pltpu.VMEM((2 * tq, 2 * LANES), jnp.bfloat16),
                        pltpu.VMEM((2 * tq, tk), jnp.float32),
                        pltpu.VMEM((2 * tq, tk), jnp.float32),
                        pltpu.VMEM((2 * tq, tk), jnp.float32),
                        pltpu.VMEM((2 * tq, LANES), jnp.float32),
                        pltpu.VMEM((2 * tq, A_V_DIM + LANES), jnp.float32)],
        compiler_params=_cparams(2),
        name="diff_attn",
    )(slopes, qkv, qkv, qkv, pos_x, kv_meta, kv_meta, pos_m, gates, lam_p, subln_w)


def _mla_attn_kernel(q_ref, k_ref, v_ref, km_ref, vm_ref, g_ref, o_ref,
                     sa_ref, sb_ref, sc_ref, m_ref, acc_ref, *, tq, tk, n_t):
    def q_rows(t):
        return q_ref[0, pl.ds(pl.multiple_of(t * tq, tq), tq), :]

    def diag_mask():
        r = lax.broadcasted_iota(jnp.int32, (tq, tk), 0)
        c = lax.broadcasted_iota(jnp.int32, (tq, tk), 1)
        return c <= r

    def q_tile(t, kind):
        row0 = pl.multiple_of(t * tq, tq)

        def scores(kt):
            k0 = pl.multiple_of(kt * tk, tk)
            return lax.dot_general(q_rows(t), k_ref[0, pl.ds(k0, tk), :], _NT,
                                   preferred_element_type=jnp.float32)

        def next_scores0():
            return lax.dot_general(q_rows(jnp.minimum(t + 1, n_t - 1)), k_ref[0, 0:tk, :], _NT,
                                   preferred_element_type=jnp.float32)

        def meta_scores():
            s = lax.dot_general(q_rows(t), km_ref[...], _NT, preferred_element_type=jnp.float32)
            cm = lax.broadcasted_iota(jnp.int32, (tq, META_PAD), 1)
            return jnp.where(cm < N_META, s, -jnp.inf)

        def finish(acc):
            o = acc[:, 0:B_V_DIM] / acc[:, B_V_DIM:B_V_DIM + LANES]
            g = g_ref[0, pl.ds(row0, tq), :]
            o_ref[0, pl.ds(row0, tq), :] = (o * _silu(g)).astype(o_ref.dtype)

        _causal_flash(t, kind, tk, scores, next_scores0, meta_scores, diag_mask, finish,
                      v_ref, vm_ref, sa_ref, sb_ref, sc_ref, m_ref, acc_ref)

    _walk_q_tiles(n_t, q_tile)


def _mla_attn(q_cat, k_cat, v, km, vm, gates, tq, tk):
    b, s, _ = q_cat.shape
    g_blk0 = A_WIDTH // LANES
    assert tq == tk and s % tq == 0
    return pl.pallas_call(
        functools.partial(_mla_attn_kernel, tq=tq, tk=tk, n_t=s // tq),
        grid=(b, B_HEADS),
        in_specs=[
            pl.BlockSpec((1, s, B_QK_PAD), lambda b, h: (b, 0, h)),
            pl.BlockSpec((1, s, B_QK_PAD), lambda b, h: (b, 0, h)),
            pl.BlockSpec((1, s, B_V_DIM), lambda b, h: (b, 0, h)),
            pl.BlockSpec((META_PAD, B_QK_PAD), lambda b, h: (0, h)),
            pl.BlockSpec((META_PAD, B_V_DIM), lambda b, h: (0, h)),
            pl.BlockSpec((1, s, LANES), lambda b, h: (b, 0, g_blk0 + h)),
        ],
        out_specs=pl.BlockSpec((1, s, B_V_DIM), lambda b, h: (b, 0, h)),
        out_shape=jax.ShapeDtypeStruct((b, s, B_WIDTH), jnp.bfloat16),
        scratch_shapes=[pltpu.VMEM((tq, tk), jnp.float32),
                        pltpu.VMEM((tq, tk), jnp.float32),
                        pltpu.VMEM((tq, tk), jnp.float32),
                        pltpu.VMEM((tq, LANES), jnp.float32),
                        pltpu.VMEM((tq, B_V_DIM + LANES), jnp.float32)],
        compiler_params=_cparams(2, MLA_ATTN_VMEM_LIMIT),
        name="mla_attn",
    )(q_cat, k_cat, v, km, vm, gates)


def _out_proj_kernel(oa_ref, ob_ref, x_ref, wa_ref, wb_ref, fw_ref, y_ref):
    d = jnp.dot(oa_ref[...], wa_ref[...], preferred_element_type=jnp.float32)
    d = d + jnp.dot(ob_ref[...], wb_ref[...], preferred_element_type=jnp.float32)
    hres = x_ref[...] + d
    r = lax.rsqrt(jnp.mean(hres * hres, axis=-1, keepdims=True) + RMS_EPS)
    y_ref[...] = (hres * r) * fw_ref[...]


def _out_proj(oa, ob, x, wa, wb, fw, tm):
    rows, d = x.shape
    full = lambda a: pl.BlockSpec(a.shape, lambda i: (0, 0))
    return pl.pallas_call(
        _out_proj_kernel,
        grid=(rows // tm,),
        in_specs=[pl.BlockSpec((tm, oa.shape[1]), lambda i: (i, 0)),
                  pl.BlockSpec((tm, ob.shape[1]), lambda i: (i, 0)),
                  pl.BlockSpec((tm, d), lambda i: (i, 0)),
                  full(wa), full(wb), full(fw)],
        out_specs=pl.BlockSpec((tm, d), lambda i: (i, 0)),
        out_shape=jax.ShapeDtypeStruct((rows, d), jnp.float32),
        compiler_params=_cparams(1),
        name="out_proj",
    )(oa, ob, x, wa, wb, fw)


def _rot_half_cols(w):
    half = w.shape[-1] // 2
    return jnp.concatenate([-w[..., half:], w[..., :half]], axis=-1)


def _pad_cols(w, width):
    return jnp.pad(w, ((0, 0), (0, width - w.shape[1])))


def kernel(x, meta_tokens, attn_norm_w, w_in, diff_lambda, diff_subln_w, mla_q_norm_w, w_uq,
           mla_kv_norm_w, w_ukv, w_out, final_norm_w):
    bsz, seq, d = x.shape
    bf16 = jnp.bfloat16
    l = 0
    lam_init = 0.8 - 0.6 * math.exp(-0.3 * l)

    wi = w_in[l]
    in_cols = (A_HEADS * 2 * A_QK_DIM, A_HEADS * 2 * A_QK_DIM, A_WIDTH, A_WIDTH,
               B_Q_LORA, B_KV_LORA, B_ROPE)
    a_q, a_k, a_v, a_g, b_cq, b_ckv, b_kr, b_g = jnp.split(
        wi, [sum(in_cols[:n + 1]) for n in range(len(in_cols))], axis=1)
    w_all = jnp.concatenate(
        [a_q * (A_QK_DIM ** -0.5 * LOG2E), a_k, a_v,
         a_g, b_g,
         b_cq, b_ckv, _pad_cols(b_kr, LANES), _pad_cols(_rot_half_cols(b_kr), LANES)],
        axis=1).astype(bf16)
    widths = (3 * A_WIDTH, A_WIDTH + B_WIDTH, SMALL_WIDTH)

    wuq = w_uq[l].reshape(B_Q_LORA, B_HEADS, B_NOPE + B_ROPE)
    wq_main = jnp.pad(wuq, ((0, 0), (0, 0), (0, B_QK_PAD - B_NOPE - B_ROPE)))
    wq_main = wq_main.reshape(B_Q_LORA, B_HEADS * B_QK_PAD).astype(bf16)
    wq_rot = jnp.pad(_rot_half_cols(wuq[..., B_NOPE:]), ((0, 0), (0, 0), (0, LANES - B_ROPE)))
    wq_rot = wq_rot.reshape(B_Q_LORA, B_HEADS * LANES).astype(bf16)
    wukv = w_ukv[l].reshape(B_KV_LORA, B_HEADS, B_NOPE + B_V_DIM)
    wk = wukv[..., :B_NOPE].reshape(B_KV_LORA, B_HEADS * B_NOPE).astype(bf16)
    wv = wukv[..., B_NOPE:].reshape(B_KV_LORA, B_WIDTH).astype(bf16)
    wo = w_out[l].astype(bf16)

    half = B_ROPE // 2
    inv = ROPE_THETA ** (-jnp.arange(half, dtype=jnp.float32) / half)
    inv = jnp.tile(inv, LANES // half)[None, :]
    slopes = jnp.exp2(-8.0 * jnp.arange(1, A_HEADS + 1, dtype=jnp.float32) / A_HEADS)

    nw = attn_norm_w[l][None, :]
    qnw = mla_q_norm_w[l][None, :]
    kvnw = mla_kv_norm_w[l][None, :]

    x2 = x.reshape(bsz * seq, d)
    w_all = w_all.reshape(d, -1, IN_PROJ_TN).transpose(1, 0, 2)
    qkv, gates, small = _in_proj(x2, nw, w_all, widths, IN_PROJ_TM)
    meta = meta_tokens.astype(x.dtype)
    qkv_m, _, small_m = _in_proj(meta, nw, w_all, widths, N_META)

    q_cat, k_cat, v_b = _mla_proj(small, qnw, kvnw, inv, wq_main, wq_rot, wk, wv, MLA_PROJ_TM,
                                  seq, N_META)
    _, k_cat_m, v_b_m = _mla_proj(small_m, qnw, kvnw, inv, wq_main, wq_rot, wk, wv, N_META,
                                  N_META, 0)

    pad_meta = lambda a: jnp.pad(a, ((0, META_PAD - N_META), (0, 0)))

    o_a = _diff_attn(slopes, qkv.reshape(bsz, seq, -1), pad_meta(qkv_m),
                     gates.reshape(bsz, seq, -1), diff_lambda[l].astype(jnp.float32),
                     diff_subln_w[l][None, :], DIFF_ATTN_T, DIFF_ATTN_T, lam_init)
    o_b = _mla_attn(q_cat.reshape(bsz, seq, -1), k_cat.reshape(bsz, seq, -1),
                    v_b.reshape(bsz, seq, -1), pad_meta(k_cat_m), pad_meta(v_b_m),
                    gates.reshape(bsz, seq, -1), MLA_ATTN_T, MLA_ATTN_T)

    y = _out_proj(o_a.reshape(bsz * seq, -1), o_b.reshape(bsz * seq, -1), x2,
                  wo[:A_WIDTH], wo[A_WIDTH:], final_norm_w[None, :], OUT_PROJ_TM)
    return y.reshape(bsz, seq, d)
```
